```python
import jax, jax.numpy as jnp
from jax import lax
import numpy as np

D_MODEL = 1024
BATCH = 8
SEQ = 4096
DEPTH = 4

N_META = 16
D_MIX = D_MODEL
D_CONV = D_MIX // 2
CONV_WIDTH = 31
N_HEADS = 8
HEAD_DIM = 64
D_ATTN = N_HEADS * HEAD_DIM
N_IDX_HEADS = 8
IDX_DIM = 64
TOPK_MAX = 256
ROPE_THETA = 10000.0
Q_BLOCK = 64
POS_BLOCK = 128
LN_EPS = 1e-5
IDX_SCALE = IDX_DIM ** -0.5
IDX_W_SCALE = N_IDX_HEADS ** -0.5
DEEPNORM_ALPHA = (2.0 * DEPTH) ** 0.25
DEEPNORM_BETA = (8.0 * DEPTH) ** -0.25

IN_SIZES = [D_CONV, D_CONV, D_CONV, D_ATTN, D_ATTN, D_ATTN, D_ATTN,
            N_IDX_HEADS * IDX_DIM, IDX_DIM, N_IDX_HEADS]
D_IN = sum(IN_SIZES)
IN_SPLITS = [sum(IN_SIZES[:i + 1]) for i in range(len(IN_SIZES) - 1)]

kernel_name = "hymba_conformer_dsa_deepnorm"


def layer_norm(x, g, b):
    xf = x.astype(jnp.float32)
    mu = jnp.mean(xf, axis=-1, keepdims=True)
    var = jnp.mean(jnp.square(xf - mu), axis=-1, keepdims=True)
    y = (xf - mu) * lax.rsqrt(var + LN_EPS) * g.astype(jnp.float32) + b.astype(jnp.float32)
    return y.astype(x.dtype)


def rope_tables(length, dim):
    inv_freq = ROPE_THETA ** (-jnp.arange(0, dim, 2, dtype=jnp.float32) / dim)
    ang = jnp.arange(length, dtype=jnp.float32)[:, None] * inv_freq[None, :]
    return jnp.cos(ang), jnp.sin(ang)


def apply_rope(x, cos, sin):
    half = x.shape[-1] // 2
    x1, x2 = x[..., :half], x[..., half:]
    out = jnp.concatenate([x1 * cos - x2 * sin, x2 * cos + x1 * sin], axis=-1)
    return out.astype(x.dtype)


def conformer_conv_branch(a, g, z, w_dw, b_dw, ln_g, ln_b):
    u = a * jax.nn.sigmoid(g)
    u = lax.conv_general_dilated(
        u, w_dw[:, None, :].astype(u.dtype), window_strides=(1,),
        padding=[(CONV_WIDTH - 1, 0)],
        dimension_numbers=('NWC', 'WIO', 'NWC'),
        feature_group_count=D_CONV) + b_dw
    u = jax.nn.silu(layer_norm(u, ln_g, ln_b))
    return u * jax.nn.silu(z)


def dsa_sparse_attention(q, k, v, qi, ki, wi, topk):
    B, L = q.shape[0], q.shape[1]
    nb = L // Q_BLOCK
    key_pos = jnp.arange(L)
    ki32 = ki.astype(jnp.float32)

    def to_blocks(t):
        return jnp.moveaxis(t.reshape((B, nb, Q_BLOCK) + t.shape[2:]), 1, 0)

    def one_block(args):
        qb, qib, wib, start = args
        q_pos = start + jnp.arange(Q_BLOCK)
        causal = key_pos[None, :] <= q_pos[:, None]
        logits = jnp.einsum('bqhd,bsd->bqhs', qib.astype(jnp.float32), ki32) * IDX_SCALE
        score = jnp.einsum('bqhs,bqh->bqs', jax.nn.relu(logits),
                           wib.astype(jnp.float32) * IDX_W_SCALE)
        score = jnp.where(causal[None], score, -jnp.inf)
        _, idx = lax.top_k(score, topk)
        valid = idx <= q_pos[None, :, None]
        kg = jax.vmap(lambda kk, ii: kk[ii])(k, idx)
        vg = jax.vmap(lambda vv, ii: vv[ii])(v, idx)
        s = jnp.einsum('bqhd,bqkhd->bqhk', qb, kg).astype(jnp.float32) * (HEAD_DIM ** -0.5)
        s = jnp.where(valid[:, :, None, :], s, -jnp.inf)
        p = jax.nn.softmax(s, axis=-1).astype(v.dtype)
        return jnp.einsum('bqhk,bqkhd->bqhd', p, vg)

    starts = jnp.arange(nb) * Q_BLOCK
    out = lax.map(one_block, (to_blocks(q), to_blocks(qi), to_blocks(wi), starts))
    return jnp.moveaxis(out, 0, 1).reshape(B, L, D_ATTN)


def setup_inputs(seed: int = 0) -> dict:
    key = jax.random.key(seed)
    ks = jax.random.split(key, 10)
    f32 = jnp.float32
    x = jax.random.normal(ks[0], (BATCH, SEQ, D_MODEL), f32)
    meta_tokens = jax.random.normal(ks[1], (N_META, D_MODEL), f32)
    w_in = jax.random.normal(ks[2], (DEPTH, D_MODEL, D_IN), f32) * D_MODEL ** -0.5
    conv_w = jax.random.normal(ks[3], (DEPTH, CONV_WIDTH, D_CONV), f32) * CONV_WIDTH ** -0.5
    conv_b = 0.01 * jax.random.normal(ks[4], (DEPTH, D_CONV), f32)
    conv_ln_g = 1.0 + 0.01 * jax.random.normal(ks[5], (DEPTH, D_CONV), f32)
    conv_ln_b = 0.01 * jax.random.normal(ks[6], (DEPTH, D_CONV), f32)
    w_out = jax.random.normal(ks[7], (DEPTH, D_MIX, D_MODEL), f32) * (D_MIX ** -0.5) * DEEPNORM_BETA
    post_ln_g = 1.0 + 0.01 * jax.random.normal(ks[8], (DEPTH, D_MODEL), f32)
    post_ln_b = 0.01 * jax.random.normal(ks[9], (DEPTH, D_MODEL), f32)
    return {"x": x, "meta_tokens": meta_tokens, "w_in": w_in, "conv_w": conv_w,
            "conv_b": conv_b, "conv_ln_g": conv_ln_g, "conv_ln_b": conv_ln_b,
            "w_out": w_out, "post_ln_g": post_ln_g, "post_ln_b": post_ln_b}


def reference(x, meta_tokens, w_in, conv_w, conv_b, conv_ln_g, conv_ln_b, w_out,
              post_ln_g, post_ln_b):
    B, S, D = x.shape
    L = S + N_META
    topk = min(TOPK_MAX, L // 4)
    Lp = -(-L // POS_BLOCK) * POS_BLOCK
    meta = jnp.broadcast_to(meta_tokens[None].astype(x.dtype), (B, N_META, D))
    h = jnp.concatenate([meta, x, jnp.zeros((B, Lp - L, D), x.dtype)], axis=1)
    cos, sin = rope_tables(Lp, HEAD_DIM)
    cos_h, sin_h = cos[:, None, :], sin[:, None, :]

    for l in range(DEPTH):
        proj = jnp.einsum('bld,de->ble', h, w_in[l])
        a, g, zc, q, k, v, za, qi, ki, wi = jnp.split(proj, IN_SPLITS, axis=-1)
        y_conv = conformer_conv_branch(a, g, zc, conv_w[l], conv_b[l], conv_ln_g[l], conv_ln_b[l])
        q = apply_rope(q.reshape(B, Lp, N_HEADS, HEAD_DIM), cos_h, sin_h)
        k = apply_rope(k.reshape(B, Lp, N_HEADS, HEAD_DIM), cos_h, sin_h)
        v = v.reshape(B, Lp, N_HEADS, HEAD_DIM)
        qi = apply_rope(qi.reshape(B, Lp, N_IDX_HEADS, IDX_DIM), cos_h, sin_h)
        ki = apply_rope(ki, cos, sin)
        y_attn = dsa_sparse_attention(q, k, v, qi, ki, wi, topk) * jax.nn.silu(za)
        y = jnp.einsum('ble,ed->bld', jnp.concatenate([y_conv, y_attn], axis=-1), w_out[l])
        h = layer_norm(DEEPNORM_ALPHA * h + y, post_ln_g[l], post_ln_b[l])

    return h[:, N_META:L]
```

```python
import functools

import jax
import jax.numpy as jnp
import numpy as np
from jax import lax
from jax.experimental import pallas as pl
from jax.experimental.pallas import tpu as pltpu

N_META = 16
CONV_WIDTH = 31
N_HEADS = 8
HEAD_DIM = 64
N_IDX_HEADS = 8
IDX_DIM = 64
TOPK_MAX = 256
ROPE_THETA = 10000.0
LN_EPS = 1e-5

LANES = 128
SLAB = 256
HEADS_PER_SLAB = SLAB // HEAD_DIM
KEY_BLOCK = 256
Q_TILE = 128
HALO = 32
VMEM_LIMIT = 56 * 1024 * 1024

NEG_BIG = -1e30
F32_NEG_INF_BITS = np.int32(-8388608)
KEY_NEG_INF = np.int32(-2139095041)
KEY_POS_INF = np.int32(2139095040)


def _layer_norm(x, g, b):
    mu = jnp.mean(x, axis=-1, keepdims=True)
    xc = x - mu
    var = jnp.mean(xc * xc, axis=-1, keepdims=True)
    return xc * lax.rsqrt(var + LN_EPS) * g + b


def _silu(x):
    return x * jax.nn.sigmoid(x)


def _rope_slab(x, cos, sin_signed, is_first_half):
    partner = jnp.where(is_first_half,
                        pltpu.roll(x, LANES - HEAD_DIM // 2, axis=1),
                        pltpu.roll(x, HEAD_DIM // 2, axis=1))
    return x * cos + partner * sin_signed


def _rope(x, cos, sin_signed, is_first_half):
    slabs = [_rope_slab(x[:, s:s + LANES], cos, sin_signed, is_first_half)
             for s in range(0, x.shape[1], LANES)]
    return jnp.concatenate(slabs, axis=1)


def _in_proj_kernel(h_ref, w_ref, cw_ref, cb_ref, lg_ref, lb_ref, cos_ref, sin_ref,
                    yc_ref, q_ref, k_ref, v_ref, ga_ref, qi_ref, ki4_ref, wi_ref,
                    ubuf_ref, *, tm, d_conv, d_attn, idx_scale):
    t = pl.program_id(1)
    x = h_ref[0].astype(jnp.bfloat16)

    def proj(c0, width):
        return jnp.dot(x, w_ref[:, c0:c0 + width], preferred_element_type=jnp.float32)

    c0 = 0
    a = proj(c0, d_conv); c0 += d_conv
    g = proj(c0, d_conv); c0 += d_conv
    u = a * jax.nn.sigmoid(g)

    @pl.when(t == 0)
    def _():
        ubuf_ref[0:HALO, :] = jnp.zeros((HALO, d_conv), jnp.float32)

    ubuf_ref[HALO:HALO + tm, :] = u
    base = HALO - (CONV_WIDTH - 1)
    conv = jnp.broadcast_to(cb_ref[...], (tm, d_conv))
    for j in range(CONV_WIDTH):
        conv = conv + cw_ref[j:j + 1, :] * ubuf_ref[base + j:base + j + tm, :]
    ubuf_ref[0:HALO, :] = u[tm - HALO:tm, :]

    zc = proj(c0, d_conv); c0 += d_conv
    yc = _silu(_layer_norm(conv, lg_ref[...], lb_ref[...])) * _silu(zc)
    yc_ref[0] = yc.astype(yc_ref.dtype)

    cos = cos_ref[...]
    sin = sin_ref[...]
    lane = lax.broadcasted_iota(jnp.int32, (tm, LANES), 1)
    first = (lane % HEAD_DIM) < (HEAD_DIM // 2)

    q = proj(c0, d_attn); c0 += d_attn
    q_ref[0] = (_rope(q, cos, sin, first) * (HEAD_DIM ** -0.5)).astype(q_ref.dtype)
    k = proj(c0, d_attn); c0 += d_attn
    k_ref[0] = _rope(k, cos, sin, first).astype(k_ref.dtype)
    v = proj(c0, d_attn); c0 += d_attn
    v_ref[0] = v.astype(v_ref.dtype)
    za = proj(c0, d_attn); c0 += d_attn
    ga_ref[0] = _silu(za)
    qi = proj(c0, d_attn); c0 += d_attn
    qi_ref[0] = _rope(qi, cos, sin, first).astype(qi_ref.dtype)
    ki4 = proj(c0, SLAB); c0 += SLAB
    ki4_ref[0] = _rope(ki4, cos, sin, first).astype(ki4_ref.dtype)
    wi = proj(c0, LANES)
    wi_ref[0] = wi * idx_scale


def _in_proj(h, w, cw, cb, lg, lb, cos, sin, *, tm, idx_scale):
    B, Lp, D = h.shape
    d_conv = cw.shape[1]
    d_attn = N_HEADS * HEAD_DIM
    nw = w.shape[1]
    grid = (B, Lp // tm)
    row = lambda width: pl.BlockSpec((1, tm, width), lambda b, t: (b, t, 0))
    full = lambda shape: pl.BlockSpec(shape, lambda b, t: (0,) * len(shape))
    bf = jnp.bfloat16
    out_shape = (
        jax.ShapeDtypeStruct((B, Lp, d_conv), bf),
        jax.ShapeDtypeStruct((B, Lp, d_attn), bf),
        jax.ShapeDtypeStruct((B, Lp, d_attn), bf),
        jax.ShapeDtypeStruct((B, Lp, d_attn), bf),
        jax.ShapeDtypeStruct((B, Lp, d_attn), jnp.float32),
        jax.ShapeDtypeStruct((B, Lp, d_attn), bf),
        jax.ShapeDtypeStruct((B, Lp, SLAB), bf),
        jax.ShapeDtypeStruct((B, Lp, LANES), jnp.float32),
    )
    out_specs = (row(d_conv), row(d_attn), row(d_attn), row(d_attn), row(d_attn),
                 row(d_attn), row(SLAB), row(LANES))
    return pl.pallas_call(
        functools.partial(_in_proj_kernel, tm=tm, d_conv=d_conv, d_attn=d_attn,
                          idx_scale=idx_scale),
        grid=grid,
        in_specs=[row(D), full((D, nw)), full(cw.shape), full(cb.shape), full(lg.shape),
                  full(lb.shape),
                  pl.BlockSpec((tm, LANES), lambda b, t: (t, 0)),
                  pl.BlockSpec((tm, LANES), lambda b, t: (t, 0))],
        out_specs=out_specs,
        out_shape=out_shape,
        scratch_shapes=[pltpu.VMEM((HALO + tm, d_conv), jnp.float32)],
        compiler_params=pltpu.CompilerParams(
            dimension_semantics=("arbitrary", "arbitrary"), vmem_limit_bytes=VMEM_LIMIT),
        name="in_proj",
    )(h, w, cw, cb, lg, lb, cos, sin)


def _key_to_f32(key):
    bits = key ^ ((key >> 31) & np.int32(0x7FFFFFFF))
    return pltpu.bitcast(bits, jnp.float32)


def _lane_group_select(vals, width):
    lane = lax.broadcasted_iota(jnp.int32, (vals[0].shape[0], width), 1)
    tiled = [jnp.concatenate([v] * (width // v.shape[1]), axis=1) if v.shape[1] != width else v
             for v in vals]
    out = tiled[-1]
    for g in range(len(vals) - 2, -1, -1):
        out = jnp.where(lane < (g + 1) * HEAD_DIM, tiled[g], out)
    return out


def _dsa_kernel(q_ref, k_ref, v_ref, qi_ref, ki4_ref, wi_ref, ga_ref, o_ref,
                sc_ref, qs_ref, qis_ref, wb_ref, thr_ref, ps_ref, m_ref, l_ref, acc_ref,
                *, tq, tk, topk):
    i = pl.program_id(1)
    row0 = i * tq
    n_chunks = (row0 + tq - 1) // tk + 1
    n_slabs = N_HEADS // HEADS_PER_SLAB

    lane = lax.broadcasted_iota(jnp.int32, (tq, SLAB), 1)
    for h in range(N_HEADS):
        c, g = divmod(h, HEADS_PER_SLAB)
        own = (lane >= g * HEAD_DIM) & (lane < (g + 1) * HEAD_DIM)
        qs_ref[c, g * tq:(g + 1) * tq, :] = jnp.where(
            own, q_ref[0, :, c * SLAB:(c + 1) * SLAB], jnp.zeros((), q_ref.dtype))
        qis_ref[h * tq:(h + 1) * tq, :] = jnp.where(
            own, qi_ref[0, :, c * SLAB:(c + 1) * SLAB], jnp.zeros((), qi_ref.dtype))
        wb_ref[h] = jnp.broadcast_to(wi_ref[0, :, h:h + 1], (tq, LANES))

    row_ids = row0 + lax.broadcasted_iota(jnp.int32, (tq, tk), 0)
    col_ids = lax.broadcasted_iota(jnp.int32, (tq, tk), 1)

    def idx_body(j, carry):
        kic = ki4_ref[0, pl.ds(pl.multiple_of(j * tk, tk), tk), :]
        logits = lax.dot_general(qis_ref[...], kic, (((1,), (1,)), ((), ())),
                                 preferred_element_type=jnp.float32)
        score = jnp.zeros((tq, tk), jnp.float32)
        for h in range(N_IDX_HEADS):
            wb = wb_ref[h]
            wbt = jnp.concatenate([wb] * (tk // LANES), axis=1)
            score = score + wbt * jnp.maximum(logits[h * tq:(h + 1) * tq, :], 0.0)
        causal = (col_ids + j * tk) <= row_ids
        sc_ref[j] = jnp.where(causal, score, -jnp.inf)
        return carry

    lax.fori_loop(0, n_chunks, idx_body, 0)

    def count_ge(thr):
        thr_b = jnp.broadcast_to(thr, (tq, LANES))

        def body(j, cnt):
            blk = sc_ref[j]
            for s in range(0, tk, LANES):
                cnt = cnt + jnp.where(blk[:, s:s + LANES] >= thr_b, 1.0, 0.0)
            return cnt

        cnt = lax.fori_loop(0, n_chunks, body, jnp.zeros((tq, LANES), jnp.float32))
        return jnp.sum(cnt, axis=1, keepdims=True)

    def bisect(_, carry):
        lo, hi, c_lo = carry
        mid = (lo >> 1) + (hi >> 1) + (lo & hi & 1)
        c = count_ge(_key_to_f32(mid))
        ge = c >= float(topk)
        return (jnp.where(ge, mid, lo), jnp.where(ge, hi, mid), jnp.where(ge, c, c_lo))

    lo0 = jnp.full((tq, 1), KEY_NEG_INF, jnp.int32)
    hi0 = jnp.full((tq, 1), KEY_POS_INF + 1, jnp.int32)
    c0 = jnp.full((tq, 1), float(topk), jnp.float32)
    lo, _, c_lo = lax.fori_loop(0, 32, bisect, (lo0, hi0, c0))
    thr_key = jnp.maximum(lo, KEY_NEG_INF + 1)
    thr = _key_to_f32(thr_key)
    thr_ref[...] = jnp.broadcast_to(thr, (tq, LANES))
    excess = jnp.where((c_lo > float(topk)) & (lo > KEY_NEG_INF), 1.0, 0.0)

    @pl.when(jnp.max(excess) > 0.0)
    def _():
        thr_b = thr_ref[...]
        thr_t = jnp.concatenate([thr_b] * (tk // LANES), axis=1)
        r_idx = lax.broadcasted_iota(jnp.int32, (tk, tk), 0)
        c_idx = lax.broadcasted_iota(jnp.int32, (tk, tk), 1)
        before = jnp.where(r_idx < c_idx, 1.0, 0.0).astype(jnp.bfloat16)

        def gt_body(j, cnt):
            blk = sc_ref[j]
            for s in range(0, tk, LANES):
                cnt = cnt + jnp.where(blk[:, s:s + LANES] > thr_b, 1.0, 0.0)
            return cnt

        n_gt = jnp.sum(lax.fori_loop(0, n_chunks, gt_body, jnp.zeros((tq, LANES), jnp.float32)),
                       axis=1, keepdims=True)
        room = float(topk) - n_gt

        def tie_body(j, seen):
            blk = sc_ref[j]
            eq = blk == thr_t
            eq_f = jnp.where(eq, 1.0, 0.0)
            prefix = jnp.dot(eq_f.astype(jnp.bfloat16), before,
                             preferred_element_type=jnp.float32)
            keep = (blk > thr_t) | (eq & ((prefix + seen) < room))
            sc_ref[j] = jnp.where(keep, 1.0, -jnp.inf)
            return seen + jnp.sum(eq_f, axis=1, keepdims=True)

        lax.fori_loop(0, n_chunks, tie_body, jnp.zeros((tq, 1), jnp.float32))
        thr_ref[...] = jnp.zeros((tq, LANES), jnp.float32)

    m_ref[...] = jnp.full(m_ref.shape, NEG_BIG, jnp.float32)
    l_ref[...] = jnp.zeros(l_ref.shape, jnp.float32)
    acc_ref[...] = jnp.zeros(acc_ref.shape, jnp.float32)
    thr_t = jnp.concatenate([thr_ref[...]] * (tk // LANES), axis=1)

    def att_body(j, carry):
        start = pl.multiple_of(j * tk, tk)
        keep = sc_ref[j] >= thr_t
        for c in range(n_slabs):
            kc = k_ref[0, pl.ds(start, tk), c * SLAB:(c + 1) * SLAB]
            vc = v_ref[0, pl.ds(start, tk), c * SLAB:(c + 1) * SLAB]
            s_all = lax.dot_general(qs_ref[c], kc, (((1,), (1,)), ((), ())),
                                    preferred_element_type=jnp.float32)
            alphas = []
            for g in range(HEADS_PER_SLAB):
                h = c * HEADS_PER_SLAB + g
                s = jnp.where(keep, s_all[g * tq:(g + 1) * tq, :], NEG_BIG)
                m_old = m_ref[h]
                m_new = jnp.maximum(m_old, jnp.max(s, axis=1, keepdims=True))
                alpha = jnp.exp(m_old - m_new)
                p = jnp.exp(s - jnp.concatenate([m_new] * (tk // LANES), axis=1))
                l_ref[h] = alpha * l_ref[h] + jnp.sum(p, axis=1, keepdims=True)
                m_ref[h] = m_new
                ps_ref[g * tq:(g + 1) * tq, :] = p.astype(ps_ref.dtype)
                alphas.append(alpha)
            r = jnp.dot(ps_ref[...], vc, preferred_element_type=jnp.float32)
            upd = _lane_group_select([r[g * tq:(g + 1) * tq, :] for g in range(HEADS_PER_SLAB)],
                                     SLAB)
            acc_ref[:, c * SLAB:(c + 1) * SLAB] = (
                _lane_group_select(alphas, SLAB) * acc_ref[:, c * SLAB:(c + 1) * SLAB] + upd)
        return carry

    lax.fori_loop(0, n_chunks, att_body, 0)

    for c in range(n_slabs):
        l_lanes = _lane_group_select(
            [l_ref[c * HEADS_PER_SLAB + g] for g in range(HEADS_PER_SLAB)], SLAB)
        out = acc_ref[:, c * SLAB:(c + 1) * SLAB] / l_lanes
        o_ref[0, :, c * SLAB:(c + 1) * SLAB] = (
            out * ga_ref[0, :, c * SLAB:(c + 1) * SLAB]).astype(o_ref.dtype)


def _dsa(q, k, v, qi, ki4, wi, ga, *, topk):
    B, Lp, d_attn = q.shape
    tq, tk = Q_TILE, KEY_BLOCK
    grid = (B, Lp // tq)
    qrow = lambda width: pl.BlockSpec((1, tq, width), lambda b, i: (b, i, 0))
    seq = lambda width: pl.BlockSpec((1, Lp, width), lambda b, i: (b, 0, 0))
    return pl.pallas_call(
        functools.partial(_dsa_kernel, tq=tq, tk=tk, topk=topk),
        grid=grid,
        in_specs=[qrow(d_attn), seq(d_attn), seq(d_attn), qrow(d_attn), seq(SLAB),
                  qrow(LANES), qrow(d_attn)],
        out_specs=qrow(d_attn),
        out_shape=jax.ShapeDtypeStruct((B, Lp, d_attn), jnp.bfloat16),
        scratch_shapes=[
            pltpu.VMEM((Lp // tk, tq, tk), jnp.float32),
            pltpu.VMEM((N_HEADS // HEADS_PER_SLAB, HEADS_PER_SLAB * tq, SLAB), jnp.bfloat16),
            pltpu.VMEM((N_IDX_HEADS * tq, SLAB), jnp.bfloat16),
            pltpu.VMEM((N_IDX_HEADS, tq, LANES), jnp.float32),
            pltpu.VMEM((tq, LANES), jnp.float32),
            pltpu.VMEM((HEADS_PER_SLAB * tq, tk), jnp.bfloat16),
            pltpu.VMEM((N_HEADS, tq, LANES), jnp.float32),
            pltpu.VMEM((N_HEADS, tq, LANES), jnp.float32),
            pltpu.VMEM((tq, d_attn), jnp.float32),
        ],
        compiler_params=pltpu.CompilerParams(
            dimension_semantics=("arbitrary", "arbitrary"), vmem_limit_bytes=VMEM_LIMIT),
        name="dsa",
    )(q, k, v, qi, ki4, wi, ga)


def _out_proj_kernel(yc_ref, ya_ref, h_ref, w_ref, g_ref, b_ref, o_ref, *, d_conv, alpha):
    y = jnp.dot(yc_ref[...], w_ref[0:d_conv, :], preferred_element_type=jnp.float32)
    y = y + jnp.dot(ya_ref[...], w_ref[d_conv:, :], preferred_element_type=jnp.float32)
    o_ref[...] = _layer_norm(alpha * h_ref[...] + y, g_ref[...], b_ref[...])


def _out_proj(yc, ya, h, w, g, b, *, tm, alpha):
    M, D = h.shape
    d_conv = yc.shape[1]
    row = lambda width: pl.BlockSpec((tm, width), lambda t: (t, 0))
    full = lambda shape: pl.BlockSpec(shape, lambda t: (0,) * len(shape))
    return pl.pallas_call(
        functools.partial(_out_proj_kernel, d_conv=d_conv, alpha=alpha),
        grid=(M // tm,),
        in_specs=[row(d_conv), row(ya.shape[1]), row(D), full(w.shape), full(g.shape),
                  full(b.shape)],
        out_specs=row(D),
        out_shape=jax.ShapeDtypeStruct((M, D), jnp.float32),
        compiler_params=pltpu.CompilerParams(
            dimension_semantics=("arbitrary",), vmem_limit_bytes=VMEM_LIMIT),
        name="out_proj",
    )(yc, ya, h, w, g, b)


def _row_tile(n, target, multiple):
    best = multiple
    for t in range(multiple, target + 1, multiple):
        if n % t == 0:
            best = t
    assert n % best == 0
    return best


def _rope_tables(length):
    inv_freq = ROPE_THETA ** (-jnp.arange(0, HEAD_DIM, 2, dtype=jnp.float32) / HEAD_DIM)
    ang = jnp.arange(length, dtype=jnp.float32)[:, None] * inv_freq[None, :]
    cos, sin = jnp.cos(ang), jnp.sin(ang)
    cos_l = jnp.tile(cos, (1, LANES // (HEAD_DIM // 2)))
    sin_l = jnp.tile(jnp.concatenate([-sin, sin], axis=1), (1, LANES // HEAD_DIM))
    return cos_l, sin_l


def kernel(x, meta_tokens, w_in, conv_w, conv_b, conv_ln_g, conv_ln_b, w_out,
           post_ln_g, post_ln_b):
    B, S, D = x.shape
    depth = w_in.shape[0]
    d_conv = conv_w.shape[2]
    d_attn = N_HEADS * HEAD_DIM
    L = S + N_META
    topk = min(TOPK_MAX, L // 4)
    Lp = -(-L // KEY_BLOCK) * KEY_BLOCK
    alpha = (2.0 * depth) ** 0.25
    idx_scale = (IDX_DIM ** -0.5) * (N_IDX_HEADS ** -0.5)

    meta = jnp.broadcast_to(meta_tokens[None].astype(x.dtype), (B, N_META, D))
    h = jnp.concatenate([meta, x, jnp.zeros((B, Lp - L, D), x.dtype)], axis=1)
    cos, sin = _rope_tables(Lp)

    n_main = 3 * d_conv + 5 * d_attn
    w_main = w_in[:, :, :n_main]
    w_ki = w_in[:, :, n_main:n_main + IDX_DIM]
    w_wi = w_in[:, :, n_main + IDX_DIM:]
    w_pack = jnp.concatenate(
        [w_main] + [w_ki] * (SLAB // IDX_DIM)
        + [w_wi, jnp.zeros((depth, D, LANES - N_IDX_HEADS), w_in.dtype)],
        axis=2).astype(jnp.bfloat16)
    w_out_b = w_out.astype(jnp.bfloat16)

    tm_in = _row_tile(Lp, 640, 16)
    tm_out = _row_tile(B * Lp, 1024, 16)

    for l in range(depth):
        yc, q, k, v, ga, qi, ki4, wi = _in_proj(
            h, w_pack[l], conv_w[l], conv_b[l][None], conv_ln_g[l][None], conv_ln_b[l][None],
            cos, sin, tm=tm_in, idx_scale=idx_scale)
        ya = _dsa(q, k, v, qi, ki4, wi, ga, topk=topk)
        h = _out_proj(yc.reshape(B * Lp, d_conv), ya.reshape(B * Lp, d_attn),
                      h.reshape(B * Lp, D), w_out_b[l], post_ln_g[l][None], post_ln_b[l][None],
                      tm=tm_out, alpha=alpha).reshape(B, Lp, D)

    return h[:, N_META:L]
```

```python
import functools

import jax
import jax.numpy as jnp
import numpy as np
from jax import lax
from jax.experimental import pallas as pl
from jax.experimental.pallas import tpu as pltpu

N_META = 16
CONV_WIDTH = 31
N_HEADS = 8
HEAD_DIM = 64
N_IDX_HEADS = 8
IDX_DIM = 64
TOPK_MAX = 256
ROPE_THETA = 10000.0
LN_EPS = 1e-5

LANES = 128
SUBLANES = 8
HEADS_PER_SLAB = LANES // HEAD_DIM
SEQ_BLOCK = 256
HALO = 32
COUNT_ROWS = 32
VMEM_LIMIT = 56 * 1024 * 1024

NEG_BIG = -1e30
KEY_NEG_INF = np.int32(-2139095041)
ZERO_CLASS_LO = np.int32(-0x00800000)
ZERO_CLASS_HI = np.int32(0x007FFFFF)
LOG2_E = 1.4426950408889634

VALUE_STEPS = 4
TIGHTEN_PERIOD = 8
MAX_SEARCH_STEPS = 56


def _layer_norm(x, g, b):
    mu = jnp.mean(x, axis=-1, keepdims=True)
    xc = x - mu
    var = jnp.mean(xc * xc, axis=-1, keepdims=True)
    return xc * lax.rsqrt(var + LN_EPS) * g + b


def _silu(x):
    return x * jax.nn.sigmoid(x)


def _rope_slab(x, cos, sin_signed, is_first_half):
    partner = jnp.where(is_first_half,
                        pltpu.roll(x, LANES - HEAD_DIM // 2, axis=1),
                        pltpu.roll(x, HEAD_DIM // 2, axis=1))
    return x * cos + partner * sin_signed


def _rope(x, cos, sin_signed, is_first_half):
    slabs = [_rope_slab(x[:, s:s + LANES], cos, sin_signed, is_first_half)
             for s in range(0, x.shape[1], LANES)]
    return jnp.concatenate(slabs, axis=1)


def _rope_t_store(xt, cos_t, sin_t, scale, out_ref):
    half = HEAD_DIM // 2
    for r0 in range(0, xt.shape[0], HEAD_DIM):
        x1 = xt[r0:r0 + half, :]
        x2 = xt[r0 + half:r0 + HEAD_DIM, :]
        out_ref[0, r0:r0 + half, :] = ((x1 * cos_t - x2 * sin_t) * scale).astype(out_ref.dtype)
        out_ref[0, r0 + half:r0 + HEAD_DIM, :] = (
            (x2 * cos_t + x1 * sin_t) * scale).astype(out_ref.dtype)


def _in_proj_kernel(h_ref, wn_ref, wt_ref, cw_ref, cb_ref, lg_ref, lb_ref,
                    cos_ref, sin_ref, cos_t_ref, sin_t_ref,
                    yc_ref, k_ref, ki2_ref, ga_ref, qt_ref, qit_ref, vt_ref, wt_out_ref,
                    ubuf_ref, ush_ref, *, tm, d_conv, d_attn, q_scale, idx_scale):
    t = pl.program_id(1)
    x = h_ref[0].astype(jnp.bfloat16)

    def proj(c0, width):
        return jnp.dot(x, wn_ref[:, c0:c0 + width], preferred_element_type=jnp.float32)

    def proj_t(r0, rows):
        return lax.dot_general(wt_ref[r0:r0 + rows, :], x, (((1,), (1,)), ((), ())),
                               preferred_element_type=jnp.float32)

    c0 = 0
    a = proj(c0, d_conv); c0 += d_conv
    g = proj(c0, d_conv); c0 += d_conv
    u = a * jax.nn.sigmoid(g)

    @pl.when(t == 0)
    def _():
        ubuf_ref[0:HALO, :] = jnp.zeros((HALO, d_conv), jnp.float32)

    ubuf_ref[HALO:HALO + tm, :] = u
    base = HALO - (CONV_WIDTH - 1)
    conv = jnp.broadcast_to(cb_ref[...], (tm, d_conv))
    for r in range(SUBLANES):
        taps = [j for j in range(CONV_WIDTH) if (base + j) % SUBLANES == r]
        if not taps:
            continue
        span = tm + ((base + taps[-1]) // SUBLANES) * SUBLANES
        ush_ref[0:span, :] = ubuf_ref[r:r + span, :]
        for j in taps:
            off = ((base + j) // SUBLANES) * SUBLANES
            conv = conv + cw_ref[j:j + 1, :] * ush_ref[off:off + tm, :]
    ubuf_ref[0:HALO, :] = u[tm - HALO:tm, :]

    zc = proj(c0, d_conv); c0 += d_conv
    yc = _silu(_layer_norm(conv, lg_ref[...], lb_ref[...])) * _silu(zc)
    yc_ref[0] = yc.astype(yc_ref.dtype)

    cos = cos_ref[...]
    sin = sin_ref[...]
    lane = lax.broadcasted_iota(jnp.int32, (tm, LANES), 1)
    first = (lane % HEAD_DIM) < (HEAD_DIM // 2)

    k = proj(c0, d_attn); c0 += d_attn
    k_ref[0] = _rope(k, cos, sin, first).astype(k_ref.dtype)
    za = proj(c0, d_attn); c0 += d_attn
    ga_ref[0] = _silu(za)
    ki2 = proj(c0, LANES)
    ki2_ref[0] = _rope_slab(ki2, cos, sin, first).astype(ki2_ref.dtype)

    cos_t = cos_t_ref[...]
    sin_t = sin_t_ref[...]
    r0 = 0
    _rope_t_store(proj_t(r0, d_attn), cos_t, sin_t, q_scale, qt_ref); r0 += d_attn
    _rope_t_store(proj_t(r0, d_attn), cos_t, sin_t, 1.0, qit_ref); r0 += d_attn
    vt_ref[0, 0] = proj_t(r0, d_attn).astype(vt_ref.dtype); r0 += d_attn
    wt_out_ref[0] = proj_t(r0, 2 * SUBLANES)[0:N_IDX_HEADS, :] * idx_scale


def _in_proj(h, wn, wt, cw, cb, lg, lb, cos, sin, cos_t, sin_t, *, q_scale, idx_scale):
    B, Lp, D = h.shape
    tm = SEQ_BLOCK
    d_conv = cw.shape[1]
    d_attn = N_HEADS * HEAD_DIM
    grid = (B, Lp // tm)
    row = lambda width: pl.BlockSpec((1, tm, width), lambda b, t: (b, t, 0))
    col = lambda rows: pl.BlockSpec((1, rows, tm), lambda b, t: (b, 0, t))
    full = lambda shape: pl.BlockSpec(shape, lambda b, t: (0,) * len(shape))
    bf = jnp.bfloat16
    out_shape = (
        jax.ShapeDtypeStruct((B, Lp, d_conv), bf),
        jax.ShapeDtypeStruct((B, Lp, d_attn), bf),
        jax.ShapeDtypeStruct((B, Lp, LANES), bf),
        jax.ShapeDtypeStruct((B, Lp, d_attn), jnp.float32),
        jax.ShapeDtypeStruct((B, d_attn, Lp), bf),
        jax.ShapeDtypeStruct((B, d_attn, Lp), bf),
        jax.ShapeDtypeStruct((B, Lp // tm, d_attn, tm), bf),
        jax.ShapeDtypeStruct((B, N_IDX_HEADS, Lp), jnp.float32),
    )
    out_specs = (row(d_conv), row(d_attn), row(LANES), row(d_attn), col(d_attn), col(d_attn),
                 pl.BlockSpec((1, 1, d_attn, tm), lambda b, t: (b, t, 0, 0)), col(N_IDX_HEADS))
    return pl.pallas_call(
        functools.partial(_in_proj_kernel, tm=tm, d_conv=d_conv, d_attn=d_attn,
                          q_scale=q_scale, idx_scale=idx_scale),
        grid=grid,
        in_specs=[row(D), full(wn.shape), full(wt.shape), full(cw.shape), full(cb.shape),
                  full(lg.shape), full(lb.shape),
                  pl.BlockSpec((tm, LANES), lambda b, t: (t, 0)),
                  pl.BlockSpec((tm, LANES), lambda b, t: (t, 0)),
                  pl.BlockSpec((HEAD_DIM // 2, tm), lambda b, t: (0, t)),
                  pl.BlockSpec((HEAD_DIM // 2, tm), lambda b, t: (0, t))],
        out_specs=out_specs,
        out_shape=out_shape,
        scratch_shapes=[pltpu.VMEM((HALO + tm, d_conv), jnp.float32),
                        pltpu.VMEM((HALO + tm, d_conv), jnp.float32)],
        compiler_params=pltpu.CompilerParams(
            dimension_semantics=("arbitrary", "arbitrary"), vmem_limit_bytes=VMEM_LIMIT),
        name="in_proj",
    )(h, wn, wt, cw, cb, lg, lb, cos, sin, cos_t, sin_t)


def _f32_to_key(x):
    bits = pltpu.bitcast(x, jnp.int32)
    return bits ^ ((bits >> 31) & np.int32(0x7FFFFFFF))


def _in_zero_class(key):
    return (key >= ZERO_CLASS_LO) & (key <= ZERO_CLASS_HI)


def _key_to_f32(key):
    val = pltpu.bitcast(key ^ ((key >> 31) & np.int32(0x7FFFFFFF)), jnp.float32)
    return jnp.where(_in_zero_class(key), 0.0, val)


def _key_succ(key):
    return jnp.where(_in_zero_class(key), ZERO_CLASS_HI + 1, key + 1)


def _fold_rows(x, op, rows):
    parts = [x[r:r + rows] for r in range(0, x.shape[0], rows)]
    while len(parts) > 1:
        nxt = [op(parts[a], parts[a + 1]) for a in range(0, len(parts) - 1, 2)]
        if len(parts) % 2:
            nxt.append(parts[-1])
        parts = nxt
    return parts[0]


def _dsa_kernel(qt_ref, qit_ref, wt_ref, k_ref, ki2_ref, vt_ref, ga_ref, o_ref,
                sc_ref, qm_ref, qim_ref, bias_ref, m_ref, l_ref, acc_ref, thr_ref,
                *, blk, topk):
    i = pl.program_id(1)
    n_chunks = i + 1
    kf = float(topk)

    srow = lax.broadcasted_iota(jnp.int32, (LANES, blk), 0)
    for h in range(N_HEADS):
        p, g = divmod(h, HEADS_PER_SLAB)
        own = (srow >= g * HEAD_DIM) & (srow < (g + 1) * HEAD_DIM)
        qm_ref[h] = jnp.where(own, qt_ref[0, p * LANES:(p + 1) * LANES, :].astype(jnp.float32),
                              0.0).astype(qm_ref.dtype)
        qim_ref[h] = jnp.where(own, qit_ref[0, p * LANES:(p + 1) * LANES, :].astype(jnp.float32),
                               0.0).astype(qim_ref.dtype)

    krow = lax.broadcasted_iota(jnp.int32, (blk, blk), 0)
    qcol = lax.broadcasted_iota(jnp.int32, (blk, blk), 1)

    def idx_body(j, carry):
        mn, mx = carry
        kic = ki2_ref[0, pl.ds(pl.multiple_of(j * blk, blk), blk), :]
        score = jnp.zeros((blk, blk), jnp.float32)
        for h in range(N_IDX_HEADS):
            logits = jnp.dot(kic, qim_ref[h], preferred_element_type=jnp.float32)
            score = score + wt_ref[0, h:h + 1, :] * jnp.maximum(logits, 0.0)
        causal = (krow + (j - i) * blk) <= qcol
        sc_ref[j] = jnp.where(causal, score, -jnp.inf)
        mn = jnp.minimum(mn, _fold_rows(jnp.where(causal, score, jnp.inf), jnp.minimum, SUBLANES))
        mx = jnp.maximum(mx, _fold_rows(jnp.where(causal, score, -jnp.inf), jnp.maximum, SUBLANES))
        return mn, mx

    mn8, mx8 = lax.fori_loop(
        0, n_chunks, idx_body,
        (jnp.full((SUBLANES, blk), jnp.inf, jnp.float32),
         jnp.full((SUBLANES, blk), -jnp.inf, jnp.float32)))
    smin = jnp.min(mn8, axis=0, keepdims=True)
    smax = jnp.max(mx8, axis=0, keepdims=True)

    def count_ge(thr_row):
        thr_b = jnp.broadcast_to(thr_row, (COUNT_ROWS, blk))

        def body(j, cnt):
            s = sc_ref[j]
            for r in range(0, blk, COUNT_ROWS):
                cnt = cnt + jnp.where(s[r:r + COUNT_ROWS] >= thr_b, 1.0, 0.0)
            return cnt

        cnt = lax.fori_loop(0, n_chunks, body, jnp.zeros((COUNT_ROWS, blk), jnp.float32))
        return jnp.sum(cnt, axis=0, keepdims=True)

    def bracket_extremes(lo_row, hi_row):
        lo_b = jnp.broadcast_to(lo_row, (SUBLANES, blk))
        hi_b = jnp.broadcast_to(hi_row, (SUBLANES, blk))

        def body(j, carry):
            a, b = carry
            s = sc_ref[j]
            for r in range(0, blk, SUBLANES):
                sr = s[r:r + SUBLANES]
                a = jnp.minimum(a, jnp.where(sr >= lo_b, sr, jnp.inf))
                b = jnp.maximum(b, jnp.where(sr < hi_b, sr, -jnp.inf))
            return a, b

        a, b = lax.fori_loop(0, n_chunks, body,
                             (jnp.full((SUBLANES, blk), jnp.inf, jnp.float32),
                              jnp.full((SUBLANES, blk), -jnp.inf, jnp.float32)))
        return jnp.min(a, axis=0, keepdims=True), jnp.max(b, axis=0, keepdims=True)

    n_valid = (i * blk + 1 + lax.broadcasted_iota(jnp.int32, (1, blk), 1)).astype(jnp.float32)
    active = n_valid > kf

    def count_step(state):
        it, lo, hi, c_lo, c_hi = state
        mid_val = _f32_to_key(0.5 * _key_to_f32(lo) + 0.5 * _key_to_f32(hi))
        mid_key = (lo >> 1) + (hi >> 1) + (lo & hi & 1)
        cand = jnp.where(it < VALUE_STEPS, mid_val, mid_key)
        cand = jnp.minimum(jnp.maximum(cand, lo + 1), hi - 1)
        c = count_ge(_key_to_f32(cand))
        ge = c >= kf
        return (it + 1, jnp.where(ge, cand, lo), jnp.where(ge, hi, cand),
                jnp.where(ge, c, c_lo), jnp.where(ge, c_hi, c))

    def tighten_step(state):
        it, lo, hi, c_lo, c_hi = state
        a, b = bracket_extremes(_key_to_f32(lo), _key_to_f32(hi))
        return (it + 1, jnp.maximum(lo, _f32_to_key(a)),
                jnp.minimum(hi, _key_succ(_f32_to_key(b))), c_lo, c_hi)

    def unresolved(state):
        _, lo, hi, c_lo, _ = state
        return jnp.where(active & (c_lo != kf) & (_key_succ(lo) < hi), 1.0, 0.0)

    def search_cond(state):
        return (state[0] < MAX_SEARCH_STEPS) & (jnp.max(unresolved(state)) > 0.0)

    def search_body(state):
        return lax.cond(state[0] % TIGHTEN_PERIOD == TIGHTEN_PERIOD - 1,
                        tighten_step, count_step, state)

    lo0 = _f32_to_key(smin)
    hi0 = _key_succ(_f32_to_key(smax))
    state = lax.while_loop(search_cond, search_body,
                           (jnp.int32(0), lo0, hi0, n_valid, jnp.zeros((1, blk), jnp.float32)))
    _, lo, _, c_lo, _ = state
    thr_row = _key_to_f32(jnp.where(active, lo, KEY_NEG_INF + 1))
    thr_ref[...] = jnp.broadcast_to(thr_row, (SUBLANES, blk))
    excess = jnp.where(active & (c_lo > kf), 1.0, 0.0)

    @pl.when(jnp.max(excess) > 0.0)
    def _():
        thr_f = jnp.broadcast_to(thr_ref[0:1, :], (blk, blk))
        thr_c = jnp.broadcast_to(thr_ref[0:1, :], (COUNT_ROWS, blk))
        earlier = jnp.where(qcol < krow, 1.0, 0.0).astype(jnp.bfloat16)

        def gt_body(j, cnt):
            s = sc_ref[j]
            for r in range(0, blk, COUNT_ROWS):
                cnt = cnt + jnp.where(s[r:r + COUNT_ROWS] > thr_c, 1.0, 0.0)
            return cnt

        n_gt = jnp.sum(lax.fori_loop(0, n_chunks, gt_body,
                                     jnp.zeros((COUNT_ROWS, blk), jnp.float32)),
                       axis=0, keepdims=True)
        room = kf - n_gt

        def tie_body(j, seen):
            s = sc_ref[j]
            eq = s == thr_f
            eq_f = jnp.where(eq, 1.0, 0.0)
            prefix = jnp.dot(earlier, eq_f.astype(jnp.bfloat16),
                             preferred_element_type=jnp.float32)
            keep = (s > thr_f) | (eq & ((prefix + seen) < room))
            sc_ref[j] = jnp.where(keep, 1.0, -jnp.inf)
            return seen + jnp.sum(eq_f, axis=0, keepdims=True)

        lax.fori_loop(0, n_chunks, tie_body, jnp.zeros((1, blk), jnp.float32))
        thr_ref[...] = jnp.zeros((SUBLANES, blk), jnp.float32)

    m_ref[...] = jnp.full(m_ref.shape, NEG_BIG, jnp.float32)
    l_ref[...] = jnp.zeros(l_ref.shape, jnp.float32)
    acc_ref[...] = jnp.zeros(acc_ref.shape, jnp.float32)
    thr_full = jnp.broadcast_to(thr_ref[0:1, :], (blk, blk))

    def all_rows(x8, op):
        return jnp.broadcast_to(op(x8, axis=0, keepdims=True), x8.shape)

    def att_body(j, carry):
        start = pl.multiple_of(j * blk, blk)
        bias_ref[...] = jnp.where(sc_ref[j] >= thr_full, 0.0, NEG_BIG)

        def qk(h):
            p = h // HEADS_PER_SLAB
            kc = k_ref[0, pl.ds(start, blk), p * LANES:(p + 1) * LANES]
            return jnp.dot(kc, qm_ref[h], preferred_element_type=jnp.float32)

        st_all = [qk(h) for h in range(N_HEADS)]
        for h in range(N_HEADS):
            st = st_all[h]
            probs, alphas = [], []
            for c0 in range(0, blk, LANES):
                s = st[:, c0:c0 + LANES] + bias_ref[:, c0:c0 + LANES]
                m_old = m_ref[h, :, c0:c0 + LANES]
                m_new = jnp.maximum(m_old, all_rows(_fold_rows(s, jnp.maximum, SUBLANES), jnp.max))
                alpha = jnp.exp2(m_old - m_new)
                pr = jnp.exp2(s - jnp.tile(m_new, (blk // SUBLANES, 1)))
                l_ref[h, :, c0:c0 + LANES] = (
                    alpha * l_ref[h, :, c0:c0 + LANES]
                    + all_rows(_fold_rows(pr, jnp.add, SUBLANES), jnp.sum))
                m_ref[h, :, c0:c0 + LANES] = m_new
                probs.append(pr.astype(jnp.bfloat16))
                alphas.append(alpha)
            o = jnp.dot(vt_ref[0, j, h * HEAD_DIM:(h + 1) * HEAD_DIM, :],
                        jnp.concatenate(probs, axis=1), preferred_element_type=jnp.float32)
            alpha_t = jnp.tile(jnp.concatenate(alphas, axis=1), (HEAD_DIM // SUBLANES, 1))
            acc_ref[h * HEAD_DIM:(h + 1) * HEAD_DIM, :] = (
                alpha_t * acc_ref[h * HEAD_DIM:(h + 1) * HEAD_DIM, :] + o)
        return carry

    lax.fori_loop(0, n_chunks, att_body, 0)

    for h in range(N_HEADS):
        acc_ref[h * HEAD_DIM:(h + 1) * HEAD_DIM, :] = (
            acc_ref[h * HEAD_DIM:(h + 1) * HEAD_DIM, :]
            / jnp.tile(l_ref[h], (HEAD_DIM // SUBLANES, 1)))
    o_ref[0] = (acc_ref[...].T * ga_ref[0]).astype(o_ref.dtype)


def _dsa(qt, qit, wt, k, ki2, vt, ga, *, topk):
    B, Lp, d_attn = k.shape
    blk = SEQ_BLOCK
    grid = (B, Lp // blk)
    qcol = lambda rows: pl.BlockSpec((1, rows, blk), lambda b, i: (b, 0, i))
    seq = lambda width: pl.BlockSpec((1, Lp, width), lambda b, i: (b, 0, 0))
    qrow = lambda width: pl.BlockSpec((1, blk, width), lambda b, i: (b, i, 0))
    return pl.pallas_call(
        functools.partial(_dsa_kernel, blk=blk, topk=topk),
        grid=grid,
        in_specs=[qcol(d_attn), qcol(d_attn), qcol(N_IDX_HEADS), seq(d_attn), seq(LANES),
                  pl.BlockSpec((1, Lp // blk, d_attn, blk), lambda b, i: (b, 0, 0, 0)),
                  qrow(d_attn)],
        out_specs=qrow(d_attn),
        out_shape=jax.ShapeDtypeStruct((B, Lp, d_attn), jnp.bfloat16),
        scratch_shapes=[
            pltpu.VMEM((Lp // blk, blk, blk), jnp.float32),
            pltpu.VMEM((N_HEADS, LANES, blk), jnp.bfloat16),
            pltpu.VMEM((N_IDX_HEADS, LANES, blk), jnp.bfloat16),
            pltpu.VMEM((blk, blk), jnp.float32),
            pltpu.VMEM((N_HEADS, SUBLANES, blk), jnp.float32),
            pltpu.VMEM((N_HEADS, SUBLANES, blk), jnp.float32),
            pltpu.VMEM((d_attn, blk), jnp.float32),
            pltpu.VMEM((SUBLANES, blk), jnp.float32),
        ],
        compiler_params=pltpu.CompilerParams(
            dimension_semantics=("arbitrary", "arbitrary"), vmem_limit_bytes=VMEM_LIMIT),
        name="dsa",
    )(qt, qit, wt, k, ki2, vt, ga)


def _out_proj_kernel(yc_ref, ya_ref, h_ref, w_ref, g_ref, b_ref, o_ref, *, d_conv, alpha):
    y = jnp.dot(yc_ref[...], w_ref[0:d_conv, :], preferred_element_type=jnp.float32)
    y = y + jnp.dot(ya_ref[...], w_ref[d_conv:, :], preferred_element_type=jnp.float32)
    o_ref[...] = _layer_norm(alpha * h_ref[...] + y, g_ref[...], b_ref[...])


def _out_proj(yc, ya, h, w, g, b, *, tm, alpha):
    M, D = h.shape
    d_conv = yc.shape[1]
    row = lambda width: pl.BlockSpec((tm, width), lambda t: (t, 0))
    full = lambda shape: pl.BlockSpec(shape, lambda t: (0,) * len(shape))
    return pl.pallas_call(
        functools.partial(_out_proj_kernel, d_conv=d_conv, alpha=alpha),
        grid=(M // tm,),
        in_specs=[row(d_conv), row(ya.shape[1]), row(D), full(w.shape), full(g.shape),
                  full(b.shape)],
        out_specs=row(D),
        out_shape=jax.ShapeDtypeStruct((M, D), jnp.float32),
        compiler_params=pltpu.CompilerParams(
            dimension_semantics=("arbitrary",), vmem_limit_bytes=VMEM_LIMIT),
        name="out_proj",
    )(yc, ya, h, w, g, b)


def _row_tile(n, target, multiple):
    best = multiple
    for t in range(multiple, target + 1, multiple):
        if n % t == 0:
            best = t
    assert n % best == 0
    return best


def _rope_tables(length):
    inv_freq = ROPE_THETA ** (-jnp.arange(0, HEAD_DIM, 2, dtype=jnp.float32) / HEAD_DIM)
    ang = jnp.arange(length, dtype=jnp.float32)[:, None] * inv_freq[None, :]
    cos, sin = jnp.cos(ang), jnp.sin(ang)
    cos_l = jnp.tile(cos, (1, LANES // (HEAD_DIM // 2)))
    sin_l = jnp.tile(jnp.concatenate([-sin, sin], axis=1), (1, LANES // HEAD_DIM))
    return cos_l, sin_l, cos.T, sin.T


def kernel(x, meta_tokens, w_in, conv_w, conv_b, conv_ln_g, conv_ln_b, w_out,
           post_ln_g, post_ln_b):
    B, S, D = x.shape
    depth = w_in.shape[0]
    d_conv = conv_w.shape[2]
    d_attn = N_HEADS * HEAD_DIM
    L = S + N_META
    topk = min(TOPK_MAX, L // 4)
    Lp = -(-L // SEQ_BLOCK) * SEQ_BLOCK
    alpha = (2.0 * depth) ** 0.25
    q_scale = (HEAD_DIM ** -0.5) * LOG2_E
    idx_scale = (IDX_DIM ** -0.5) * (N_IDX_HEADS ** -0.5)

    meta = jnp.broadcast_to(meta_tokens[None].astype(x.dtype), (B, N_META, D))
    h = jnp.concatenate([meta, x, jnp.zeros((B, Lp - L, D), x.dtype)], axis=1)
    cos, sin, cos_t, sin_t = _rope_tables(Lp)

    bounds = np.cumsum([0, d_conv, d_conv, d_conv, d_attn, d_attn, d_attn, d_attn,
                        N_IDX_HEADS * IDX_DIM, IDX_DIM, N_IDX_HEADS])
    part = lambda n: w_in[:, :, bounds[n]:bounds[n + 1]]
    bf = jnp.bfloat16
    w_nat = jnp.concatenate(
        [part(0), part(1), part(2), part(4), part(6)] + [part(8)] * (LANES // IDX_DIM),
        axis=2).astype(bf)
    w_tr = jnp.swapaxes(jnp.concatenate(
        [part(3), part(7), part(5), part(9),
         jnp.zeros((depth, D, 2 * SUBLANES - N_IDX_HEADS), w_in.dtype)], axis=2), 1, 2).astype(bf)
    w_out_b = w_out.astype(bf)

    tm_out = _row_tile(B * Lp, 1024, 16)

    for l in range(depth):
        yc, k, ki2, ga, qt, qit, vt, wt = _in_proj(
            h, w_nat[l], w_tr[l], conv_w[l], conv_b[l][None], conv_ln_g[l][None],
            conv_ln_b[l][None], cos, sin, cos_t, sin_t, q_scale=q_scale, idx_scale=idx_scale)
        ya = _dsa(qt, qit, wt, k, ki2, vt, ga, topk=topk)
        h = _out_proj(yc.reshape(B * Lp, d_conv), ya.reshape(B * Lp, d_attn),
                      h.reshape(B * Lp, D), w_out_b[l], post_ln_g[l][None], post_ln_b[l][None],
                      tm=tm_out, alpha=alpha).reshape(B, Lp, D)

    return h[:, N_META:L]
```

```python
import functools

import jax
import jax.numpy as jnp
import numpy as np
from jax import lax
from jax.experimental import pallas as pl
from jax.experimental.pallas import tpu as pltpu

N_META = 16
CONV_WIDTH = 31
N_HEADS = 8
HEAD_DIM = 64
N_IDX_HEADS = 8
IDX_DIM = 64
TOPK_MAX = 256
ROPE_THETA = 10000.0
LN_EPS = 1e-5

LANES = 128
SUBLANES = 8
HEADS_PER_SLAB = LANES // HEAD_DIM
SEQ_BLOCK = 256
HALO = 32
COUNT_ROWS = 32
VMEM_LIMIT = 56 * 1024 * 1024

NEG_BIG = -1e30
KEY_NEG_INF = np.int32(-2139095041)
ZERO_CLASS_LO = np.int32(-0x00800000)
ZERO_CLASS_HI = np.int32(0x007FFFFF)
LOG2_E = 1.4426950408889634

SEARCH_PERIOD = 8
KEY_STEP = 3
TIGHTEN_STEP = 7
MAX_SEARCH_STEPS = 8 * 34


def _layer_norm(x, g, b):
    mu = jnp.mean(x, axis=-1, keepdims=True)
    xc = x - mu
    var = jnp.mean(xc * xc, axis=-1, keepdims=True)
    return xc * lax.rsqrt(var + LN_EPS) * g + b


def _silu(x):
    return x * jax.nn.sigmoid(x)


def _rope_slab(x, cos, sin_signed, is_first_half):
    partner = jnp.where(is_first_half,
                        pltpu.roll(x, LANES - HEAD_DIM // 2, axis=1),
                        pltpu.roll(x, HEAD_DIM // 2, axis=1))
    return x * cos + partner * sin_signed


def _rope(x, cos, sin_signed, is_first_half):
    slabs = [_rope_slab(x[:, s:s + LANES], cos, sin_signed, is_first_half)
             for s in range(0, x.shape[1], LANES)]
    return jnp.concatenate(slabs, axis=1)


def _rope_t_store(xt, cos_t, sin_t, scale, out_ref):
    half = HEAD_DIM // 2
    for r0 in range(0, xt.shape[0], HEAD_DIM):
        x1 = xt[r0:r0 + half, :]
        x2 = xt[r0 + half:r0 + HEAD_DIM, :]
        out_ref[0, r0:r0 + half, :] = ((x1 * cos_t - x2 * sin_t) * scale).astype(out_ref.dtype)
        out_ref[0, r0 + half:r0 + HEAD_DIM, :] = (
            (x2 * cos_t + x1 * sin_t) * scale).astype(out_ref.dtype)


def _in_proj_kernel(h_ref, wn_ref, wt_ref, cw_ref, cb_ref, lg_ref, lb_ref,
                    cos_ref, sin_ref, cos_t_ref, sin_t_ref,
                    yc_ref, k_ref, ki2_ref, ga_ref, qt_ref, qit_ref, vt_ref, wt_out_ref,
                    ubuf_ref, ush_ref, *, tm, d_conv, d_attn, q_scale, idx_scale):
    t = pl.program_id(1)
    x = h_ref[0].astype(jnp.bfloat16)

    def proj(c0, width):
        return jnp.dot(x, wn_ref[:, c0:c0 + width], preferred_element_type=jnp.float32)

    def proj_t(r0, rows):
        return lax.dot_general(wt_ref[r0:r0 + rows, :], x, (((1,), (1,)), ((), ())),
                               preferred_element_type=jnp.float32)

    c0 = 0
    a = proj(c0, d_conv); c0 += d_conv
    g = proj(c0, d_conv); c0 += d_conv
    u = a * jax.nn.sigmoid(g)

    @pl.when(t == 0)
    def _():
        ubuf_ref[0:HALO, :] = jnp.zeros((HALO, d_conv), jnp.float32)

    ubuf_ref[HALO:HALO + tm, :] = u
    base = HALO - (CONV_WIDTH - 1)
    conv = jnp.broadcast_to(cb_ref[...], (tm, d_conv))
    for r in range(SUBLANES):
        taps = [j for j in range(CONV_WIDTH) if (base + j) % SUBLANES == r]
        if not taps:
            continue
        span = tm + ((base + taps[-1]) // SUBLANES) * SUBLANES
        ush_ref[0:span, :] = ubuf_ref[r:r + span, :]
        for j in taps:
            off = ((base + j) // SUBLANES) * SUBLANES
            conv = conv + cw_ref[j:j + 1, :] * ush_ref[off:off + tm, :]
    ubuf_ref[0:HALO, :] = u[tm - HALO:tm, :]

    zc = proj(c0, d_conv); c0 += d_conv
    yc = _silu(_layer_norm(conv, lg_ref[...], lb_ref[...])) * _silu(zc)
    yc_ref[0] = yc.astype(yc_ref.dtype)

    cos = cos_ref[...]
    sin = sin_ref[...]
    lane = lax.broadcasted_iota(jnp.int32, (tm, LANES), 1)
    first = (lane % HEAD_DIM) < (HEAD_DIM // 2)

    k = proj(c0, d_attn); c0 += d_attn
    k_ref[0] = _rope(k, cos, sin, first).astype(k_ref.dtype)
    za = proj(c0, d_attn); c0 += d_attn
    ga_ref[0] = _silu(za)
    ki2 = proj(c0, LANES)
    ki2_ref[0] = _rope_slab(ki2, cos, sin, first).astype(ki2_ref.dtype)

    cos_t = cos_t_ref[...]
    sin_t = sin_t_ref[...]
    r0 = 0
    _rope_t_store(proj_t(r0, d_attn), cos_t, sin_t, q_scale, qt_ref); r0 += d_attn
    _rope_t_store(proj_t(r0, d_attn), cos_t, sin_t, 1.0, qit_ref); r0 += d_attn
    vt_ref[0, 0] = proj_t(r0, d_attn).astype(vt_ref.dtype); r0 += d_attn
    wt_out_ref[0] = proj_t(r0, 2 * SUBLANES)[0:N_IDX_HEADS, :] * idx_scale


def _in_proj(h, wn, wt, cw, cb, lg, lb, cos, sin, cos_t, sin_t, *, q_scale, idx_scale):
    B, Lp, D = h.shape
    tm = SEQ_BLOCK
    d_conv = cw.shape[1]
    d_attn = N_HEADS * HEAD_DIM
    grid = (B, Lp // tm)
    row = lambda width: pl.BlockSpec((1, tm, width), lambda b, t: (b, t, 0))
    col = lambda rows: pl.BlockSpec((1, rows, tm), lambda b, t: (b, 0, t))
    full = lambda shape: pl.BlockSpec(shape, lambda b, t: (0,) * len(shape))
    bf = jnp.bfloat16
    out_shape = (
        jax.ShapeDtypeStruct((B, Lp, d_conv), bf),
        jax.ShapeDtypeStruct((B, Lp, d_attn), bf),
        jax.ShapeDtypeStruct((B, Lp, LANES), bf),
        jax.ShapeDtypeStruct((B, Lp, d_attn), jnp.float32),
        jax.ShapeDtypeStruct((B, d_attn, Lp), bf),
        jax.ShapeDtypeStruct((B, d_attn, Lp), bf),
        jax.ShapeDtypeStruct((B, Lp // tm, d_attn, tm), bf),
        jax.ShapeDtypeStruct((B, N_IDX_HEADS, Lp), jnp.float32),
    )
    out_specs = (row(d_conv), row(d_attn), row(LANES), row(d_attn), col(d_attn), col(d_attn),
                 pl.BlockSpec((1, 1, d_attn, tm), lambda b, t: (b, t, 0, 0)), col(N_IDX_HEADS))
    return pl.pallas_call(
        functools.partial(_in_proj_kernel, tm=tm, d_conv=d_conv, d_attn=d_attn,
                          q_scale=q_scale, idx_scale=idx_scale),
        grid=grid,
        in_specs=[row(D), full(wn.shape), full(wt.shape), full(cw.shape), full(cb.shape),
                  full(lg.shape), full(lb.shape),
                  pl.BlockSpec((tm, LANES), lambda b, t: (t, 0)),
                  pl.BlockSpec((tm, LANES), lambda b, t: (t, 0)),
                  pl.BlockSpec((HEAD_DIM // 2, tm), lambda b, t: (0, t)),
                  pl.BlockSpec((HEAD_DIM // 2, tm), lambda b, t: (0, t))],
        out_specs=out_specs,
        out_shape=out_shape,
        scratch_shapes=[pltpu.VMEM((HALO + tm, d_conv), jnp.float32),
                        pltpu.VMEM((HALO + tm, d_conv), jnp.float32)],
        compiler_params=pltpu.CompilerParams(
            dimension_semantics=("arbitrary", "arbitrary"), vmem_limit_bytes=VMEM_LIMIT),
        name="in_proj",
    )(h, wn, wt, cw, cb, lg, lb, cos, sin, cos_t, sin_t)


def _f32_to_key(x):
    bits = pltpu.bitcast(x, jnp.int32)
    return bits ^ ((bits >> 31) & np.int32(0x7FFFFFFF))


def _in_zero_class(key):
    return (key >= ZERO_CLASS_LO) & (key <= ZERO_CLASS_HI)


def _key_to_f32(key):
    val = pltpu.bitcast(key ^ ((key >> 31) & np.int32(0x7FFFFFFF)), jnp.float32)
    return jnp.where(_in_zero_class(key), 0.0, val)


def _key_succ(key):
    return jnp.where(_in_zero_class(key), ZERO_CLASS_HI + 1, key + 1)


def _fold_rows(x, op, rows):
    parts = [x[r:r + rows] for r in range(0, x.shape[0], rows)]
    while len(parts) > 1:
        nxt = [op(parts[a], parts[a + 1]) for a in range(0, len(parts) - 1, 2)]
        if len(parts) % 2:
            nxt.append(parts[-1])
        parts = nxt
    return parts[0]


def _dsa_kernel(qt_ref, qit_ref, wt_ref, k_ref, ki2_ref, vt_ref, ga_ref, o_ref,
                sc_ref, qm_ref, qim_ref, bias_ref, m_ref, l_ref, acc_ref, thr_ref, sta_ref, stb_ref,
                *, blk, topk):
    i = pl.program_id(1)
    n_chunks = i + 1
    kf = float(topk)

    srow = lax.broadcasted_iota(jnp.int32, (LANES, blk), 0)
    for h in range(N_HEADS):
        p, g = divmod(h, HEADS_PER_SLAB)
        own = (srow >= g * HEAD_DIM) & (srow < (g + 1) * HEAD_DIM)
        qm_ref[h] = jnp.where(own, qt_ref[0, p * LANES:(p + 1) * LANES, :].astype(jnp.float32),
                              0.0).astype(qm_ref.dtype)
        qim_ref[h] = jnp.where(own, qit_ref[0, p * LANES:(p + 1) * LANES, :].astype(jnp.float32),
                               0.0).astype(qim_ref.dtype)

    krow = lax.broadcasted_iota(jnp.int32, (blk, blk), 0)
    qcol = lax.broadcasted_iota(jnp.int32, (blk, blk), 1)

    def idx_body(j, carry):
        mn, mx = carry
        kic = ki2_ref[0, pl.ds(pl.multiple_of(j * blk, blk), blk), :]
        logits = [jnp.dot(kic, qim_ref[h], preferred_element_type=jnp.float32)
                  for h in range(N_IDX_HEADS)]
        score = jnp.zeros((blk, blk), jnp.float32)
        for h in range(N_IDX_HEADS):
            score = score + wt_ref[0, h:h + 1, :] * jnp.maximum(logits[h], 0.0)
        causal = (krow + (j - i) * blk) <= qcol
        sc_ref[j] = jnp.where(causal, score, -jnp.inf)
        mn = jnp.minimum(mn, _fold_rows(jnp.where(causal, score, jnp.inf), jnp.minimum, SUBLANES))
        mx = jnp.maximum(mx, _fold_rows(jnp.where(causal, score, -jnp.inf), jnp.maximum, SUBLANES))
        return mn, mx

    mn8, mx8 = lax.fori_loop(
        0, n_chunks, idx_body,
        (jnp.full((SUBLANES, blk), jnp.inf, jnp.float32),
         jnp.full((SUBLANES, blk), -jnp.inf, jnp.float32)))
    smin = jnp.min(mn8, axis=0, keepdims=True)
    smax = jnp.max(mx8, axis=0, keepdims=True)

    def count_ge(thr_row):
        thr_b = jnp.broadcast_to(thr_row, (COUNT_ROWS, blk))

        def body(j, cnt):
            s = sc_ref[j]
            for r in range(0, blk, COUNT_ROWS):
                cnt = cnt + jnp.where(s[r:r + COUNT_ROWS] >= thr_b, 1.0, 0.0)
            return cnt

        cnt = lax.fori_loop(0, n_chunks, body, jnp.zeros((COUNT_ROWS, blk), jnp.float32))
        return jnp.sum(cnt, axis=0, keepdims=True)

    def bracket_extremes(lo_row, hi_row):
        lo_b = jnp.broadcast_to(lo_row, (SUBLANES, blk))
        hi_b = jnp.broadcast_to(hi_row, (SUBLANES, blk))

        def body(j, carry):
            a, b = carry
            s = sc_ref[j]
            for r in range(0, blk, SUBLANES):
                sr = s[r:r + SUBLANES]
                a = jnp.minimum(a, jnp.where(sr >= lo_b, sr, jnp.inf))
                b = jnp.maximum(b, jnp.where(sr < hi_b, sr, -jnp.inf))
            return a, b

        a, b = lax.fori_loop(0, n_chunks, body,
                             (jnp.full((SUBLANES, blk), jnp.inf, jnp.float32),
                              jnp.full((SUBLANES, blk), -jnp.inf, jnp.float32)))
        return jnp.min(a, axis=0, keepdims=True), jnp.max(b, axis=0, keepdims=True)

    n_valid = (i * blk + 1 + lax.broadcasted_iota(jnp.int32, (1, blk), 1)).astype(jnp.float32)
    active = n_valid > kf

    def count_step(state):
        it, lo, hi, c_lo, c_hi = state
        mid_val = _f32_to_key(0.5 * _key_to_f32(lo) + 0.5 * _key_to_f32(hi))
        mid_key = (lo >> 1) + (hi >> 1) + (lo & hi & 1)
        cand = jnp.where(it % SEARCH_PERIOD == KEY_STEP, mid_key, mid_val)
        cand = jnp.minimum(jnp.maximum(cand, lo + 1), hi - 1)
        c = count_ge(_key_to_f32(cand))
        ge = c >= kf
        return (it + 1, jnp.where(ge, cand, lo), jnp.where(ge, hi, cand),
                jnp.where(ge, c, c_lo), jnp.where(ge, c_hi, c))

    def tighten_step(state):
        it, lo, hi, c_lo, c_hi = state
        a, b = bracket_extremes(_key_to_f32(lo), _key_to_f32(hi))
        return (it + 1, jnp.maximum(lo, _f32_to_key(a)),
                jnp.minimum(hi, _key_succ(_f32_to_key(b))), c_lo, c_hi)

    def unresolved(state):
        _, lo, hi, c_lo, _ = state
        return jnp.where(active & (c_lo != kf) & (_key_succ(lo) < hi), 1.0, 0.0)

    def search_cond(state):
        return (state[0] < MAX_SEARCH_STEPS) & (jnp.max(unresolved(state)) > 0.0)

    def search_step(state):
        return lax.cond(state[0] % SEARCH_PERIOD == TIGHTEN_STEP, tighten_step, count_step, state)

    def search_body(state):
        return search_step(search_step(state))

    lo0 = _f32_to_key(smin)
    hi0 = _key_succ(_f32_to_key(smax))
    state = lax.while_loop(search_cond, search_body,
                           (jnp.int32(0), lo0, hi0, n_valid, jnp.zeros((1, blk), jnp.float32)))
    _, lo, _, c_lo, _ = state
    thr_row = _key_to_f32(jnp.where(active, lo, KEY_NEG_INF + 1))
    thr_ref[...] = jnp.broadcast_to(thr_row, (SUBLANES, blk))
    excess = jnp.where(active & (c_lo > kf), 1.0, 0.0)

    @pl.when(jnp.max(excess) > 0.0)
    def _():
        thr_f = jnp.broadcast_to(thr_ref[0:1, :], (blk, blk))
        thr_c = jnp.broadcast_to(thr_ref[0:1, :], (COUNT_ROWS, blk))
        earlier = jnp.where(qcol < krow, 1.0, 0.0).astype(jnp.bfloat16)

        def gt_body(j, cnt):
            s = sc_ref[j]
            for r in range(0, blk, COUNT_ROWS):
                cnt = cnt + jnp.where(s[r:r + COUNT_ROWS] > thr_c, 1.0, 0.0)
            return cnt

        n_gt = jnp.sum(lax.fori_loop(0, n_chunks, gt_body,
                                     jnp.zeros((COUNT_ROWS, blk), jnp.float32)),
                       axis=0, keepdims=True)
        room = kf - n_gt

        def tie_body(j, seen):
            s = sc_ref[j]
            eq = s == thr_f
            eq_f = jnp.where(eq, 1.0, 0.0)
            prefix = jnp.dot(earlier, eq_f.astype(jnp.bfloat16),
                             preferred_element_type=jnp.float32)
            keep = (s > thr_f) | (eq & ((prefix + seen) < room))
            sc_ref[j] = jnp.where(keep, 1.0, -jnp.inf)
            return seen + jnp.sum(eq_f, axis=0, keepdims=True)

        lax.fori_loop(0, n_chunks, tie_body, jnp.zeros((1, blk), jnp.float32))
        thr_ref[...] = jnp.zeros((SUBLANES, blk), jnp.float32)

    m_ref[...] = jnp.full(m_ref.shape, NEG_BIG, jnp.float32)
    l_ref[...] = jnp.zeros(l_ref.shape, jnp.float32)
    acc_ref[...] = jnp.zeros(acc_ref.shape, jnp.float32)
    thr_full = jnp.broadcast_to(thr_ref[0:1, :], (blk, blk))

    def all_rows(x8, op):
        return jnp.broadcast_to(op(x8, axis=0, keepdims=True), x8.shape)

    def qk(h, j):
        p = h // HEADS_PER_SLAB
        kc = k_ref[0, pl.ds(pl.multiple_of(j * blk, blk), blk), p * LANES:(p + 1) * LANES]
        return jnp.dot(kc, qm_ref[h], preferred_element_type=jnp.float32)

    def attend(j, thr_bump, st_cur, st_next, j_next):
        bias_ref[...] = jnp.where(sc_ref[j] >= thr_full + thr_bump, 0.0, NEG_BIG)
        for h in range(N_HEADS):
            st_next[h] = qk(h, j_next)
            probs, alphas = [], []
            for c0 in range(0, blk, LANES):
                s = st_cur[h, :, c0:c0 + LANES] + bias_ref[:, c0:c0 + LANES]
                m_old = m_ref[h, :, c0:c0 + LANES]
                m_new = jnp.maximum(m_old, all_rows(_fold_rows(s, jnp.maximum, SUBLANES), jnp.max))
                alpha = jnp.exp2(m_old - m_new)
                pr = jnp.exp2(s - jnp.tile(m_new, (blk // SUBLANES, 1)))
                l_ref[h, :, c0:c0 + LANES] = (
                    alpha * l_ref[h, :, c0:c0 + LANES]
                    + all_rows(_fold_rows(pr, jnp.add, SUBLANES), jnp.sum))
                m_ref[h, :, c0:c0 + LANES] = m_new
                probs.append(pr.astype(jnp.bfloat16))
                alphas.append(alpha)
            o = jnp.dot(vt_ref[0, j, h * HEAD_DIM:(h + 1) * HEAD_DIM, :],
                        jnp.concatenate(probs, axis=1), preferred_element_type=jnp.float32)
            alpha_t = jnp.tile(jnp.concatenate(alphas, axis=1), (HEAD_DIM // SUBLANES, 1))
            acc_ref[h * HEAD_DIM:(h + 1) * HEAD_DIM, :] = (
                alpha_t * acc_ref[h * HEAD_DIM:(h + 1) * HEAD_DIM, :] + o)

    last = n_chunks - 1
    for h in range(N_HEADS):
        sta_ref[h] = qk(h, 0)

    def pair_body(pair, carry):
        j0 = 2 * pair
        j1 = j0 + 1
        attend(j0, 0.0, sta_ref, stb_ref, jnp.minimum(j1, last))
        attend(jnp.minimum(j1, last), jnp.where(j1 <= last, 0.0, jnp.inf), stb_ref, sta_ref,
               jnp.minimum(j1 + 1, last))
        return carry

    lax.fori_loop(0, (n_chunks + 1) // 2, pair_body, 0)

    for h in range(N_HEADS):
        acc_ref[h * HEAD_DIM:(h + 1) * HEAD_DIM, :] = (
            acc_ref[h * HEAD_DIM:(h + 1) * HEAD_DIM, :]
            / jnp.tile(l_ref[h], (HEAD_DIM // SUBLANES, 1)))
    o_ref[0] = (acc_ref[...].T * ga_ref[0]).astype(o_ref.dtype)


def _dsa(qt, qit, wt, k, ki2, vt, ga, *, topk):
    B, Lp, d_attn = k.shape
    blk = SEQ_BLOCK
    grid = (B, Lp // blk)
    qcol = lambda rows: pl.BlockSpec((1, rows, blk), lambda b, i: (b, 0, i))
    seq = lambda width: pl.BlockSpec((1, Lp, width), lambda b, i: (b, 0, 0))
    qrow = lambda width: pl.BlockSpec((1, blk, width), lambda b, i: (b, i, 0))
    return pl.pallas_call(
        functools.partial(_dsa_kernel, blk=blk, topk=topk),
        grid=grid,
        in_specs=[qcol(d_attn), qcol(d_attn), qcol(N_IDX_HEADS), seq(d_attn), seq(LANES),
                  pl.BlockSpec((1, Lp // blk, d_attn, blk), lambda b, i: (b, 0, 0, 0)),
                  qrow(d_attn)],
        out_specs=qrow(d_attn),
        out_shape=jax.ShapeDtypeStruct((B, Lp, d_attn), jnp.bfloat16),
        scratch_shapes=[
            pltpu.VMEM((Lp // blk, blk, blk), jnp.float32),
            pltpu.VMEM((N_HEADS, LANES, blk), jnp.bfloat16),
            pltpu.VMEM((N_IDX_HEADS, LANES, blk), jnp.bfloat16),
            pltpu.VMEM((blk, blk), jnp.float32),
            pltpu.VMEM((N_HEADS, SUBLANES, blk), jnp.float32),
            pltpu.VMEM((N_HEADS, SUBLANES, blk), jnp.float32),
            pltpu.VMEM((d_attn, blk), jnp.float32),
            pltpu.VMEM((SUBLANES, blk), jnp.float32),
            pltpu.VMEM((N_HEADS, blk, blk), jnp.float32),
            pltpu.VMEM((N_HEADS, blk, blk), jnp.float32),
        ],
        compiler_params=pltpu.CompilerParams(
            dimension_semantics=("arbitrary", "arbitrary"), vmem_limit_bytes=VMEM_LIMIT),
        name="dsa",
    )(qt, qit, wt, k, ki2, vt, ga)


def _out_proj_kernel(yc_ref, ya_ref, h_ref, w_ref, g_ref, b_ref, o_ref, *, d_conv, alpha):
    y = jnp.dot(yc_ref[...], w_ref[0:d_conv, :], preferred_element_type=jnp.float32)
    y = y + jnp.dot(ya_ref[...], w_ref[d_conv:, :], preferred_element_type=jnp.float32)
    o_ref[...] = _layer_norm(alpha * h_ref[...] + y, g_ref[...], b_ref[...])


def _out_proj(yc, ya, h, w, g, b, *, tm, alpha):
    M, D = h.shape
    d_conv = yc.shape[1]
    row = lambda width: pl.BlockSpec((tm, width), lambda t: (t, 0))
    full = lambda shape: pl.BlockSpec(shape, lambda t: (0,) * len(shape))
    return pl.pallas_call(
        functools.partial(_out_proj_kernel, d_conv=d_conv, alpha=alpha),
        grid=(M // tm,),
        in_specs=[row(d_conv), row(ya.shape[1]), row(D), full(w.shape), full(g.shape),
                  full(b.shape)],
        out_specs=row(D),
        out_shape=jax.ShapeDtypeStruct((M, D), jnp.float32),
        compiler_params=pltpu.CompilerParams(
            dimension_semantics=("arbitrary",), vmem_limit_bytes=VMEM_LIMIT),
        name="out_proj",
    )(yc, ya, h, w, g, b)


def _row_tile(n, target, multiple):
    best = multiple
    for t in range(multiple, target + 1, multiple):
        if n % t == 0:
            best = t
    assert n % best == 0
    return best


def _rope_tables(length):
    inv_freq = ROPE_THETA ** (-jnp.arange(0, HEAD_DIM, 2, dtype=jnp.float32) / HEAD_DIM)
    ang = jnp.arange(length, dtype=jnp.float32)[:, None] * inv_freq[None, :]
    cos, sin = jnp.cos(ang), jnp.sin(ang)
    cos_l = jnp.tile(cos, (1, LANES // (HEAD_DIM // 2)))
    sin_l = jnp.tile(jnp.concatenate([-sin, sin], axis=1), (1, LANES // HEAD_DIM))
    return cos_l, sin_l, cos.T, sin.T


def kernel(x, meta_tokens, w_in, conv_w, conv_b, conv_ln_g, conv_ln_b, w_out,
           post_ln_g, post_ln_b):
    B, S, D = x.shape
    depth = w_in.shape[0]
    d_conv = conv_w.shape[2]
    d_attn = N_HEADS * HEAD_DIM
    L = S + N_META
    topk = min(TOPK_MAX, L // 4)
    Lp = -(-L // SEQ_BLOCK) * SEQ_BLOCK
    alpha = (2.0 * depth) ** 0.25
    q_scale = (HEAD_DIM ** -0.5) * LOG2_E
    idx_scale = (IDX_DIM ** -0.5) * (N_IDX_HEADS ** -0.5)

    meta = jnp.broadcast_to(meta_tokens[None].astype(x.dtype), (B, N_META, D))
    h = jnp.concatenate([meta, x, jnp.zeros((B, Lp - L, D), x.dtype)], axis=1)
    cos, sin, cos_t, sin_t = _rope_tables(Lp)

    bounds = np.cumsum([0, d_conv, d_conv, d_conv, d_attn, d_attn, d_attn, d_attn,
                        N_IDX_HEADS * IDX_DIM, IDX_DIM, N_IDX_HEADS])
    part = lambda n: w_in[:, :, bounds[n]:bounds[n + 1]]
    bf = jnp.bfloat16
    w_nat = jnp.concatenate(
        [part(0), part(1), part(2), part(4), part(6)] + [part(8)] * (LANES // IDX_DIM),
        axis=2).astype(bf)
    w_tr = jnp.swapaxes(jnp.concatenate(
        [part(3), part(7), part(5), part(9),
         jnp.zeros((depth, D, 2 * SUBLANES - N_IDX_HEADS), w_in.dtype)], axis=2), 1, 2).astype(bf)
    w_out_b = w_out.astype(bf)

    tm_out = _row_tile(B * Lp, 1024, 16)

    for l in range(depth):
        yc, k, ki2, ga, qt, qit, vt, wt = _in_proj(
            h, w_nat[l], w_tr[l], conv_w[l], conv_b[l][None], conv_ln_g[l][None],
            conv_ln_b[l][None], cos, sin, cos_t, sin_t, q_scale=q_scale, idx_scale=idx_scale)
        ya = _dsa(qt, qit, wt, k, ki2, vt, ga, topk=topk)
        h = _out_proj(yc.reshape(B * Lp, d_conv), ya.reshape(B * Lp, d_attn),
                      h.reshape(B * Lp, D), w_out_b[l], post_ln_g[l][None], post_ln_b[l][None],
                      tm=tm_out, alpha=alpha).reshape(B, Lp, D)

    return h[:, N_META:L]
```

```python
import functools

import jax
import jax.numpy as jnp
import numpy as np
from jax import lax
from jax.experimental import pallas as pl
from jax.experimental.pallas import tpu as pltpu

N_META = 16
CONV_WIDTH = 31
N_HEADS = 8
HEAD_DIM = 64
N_IDX_HEADS = 8
IDX_DIM = 64
TOPK_MAX = 256
ROPE_THETA = 10000.0
LN_EPS = 1e-5

LANES = 128
SUBLANES = 8
HEADS_PER_SLAB = LANES // HEAD_DIM
SEQ_BLOCK = 256
HALO = 32
COUNT_ROWS = 32
VMEM_LIMIT = 56 * 1024 * 1024

NEG_BIG = -1e30
KEY_NEG_INF = np.int32(-2139095041)
ZERO_CLASS_LO = np.int32(-0x00800000)
ZERO_CLASS_HI = np.int32(0x007FFFFF)
LOG2_E = 1.4426950408889634

SEARCH_PERIOD = 8
KEY_STEP = 3
TIGHTEN_STEP = 7
MAX_SEARCH_STEPS = 8 * 34
MIN_SEARCH_STEPS = 16


def _layer_norm(x, g, b):
    mu = jnp.mean(x, axis=-1, keepdims=True)
    xc = x - mu
    var = jnp.mean(xc * xc, axis=-1, keepdims=True)
    return xc * lax.rsqrt(var + LN_EPS) * g + b


def _silu(x):
    return x * jax.nn.sigmoid(x)


def _rope_slab(x, cos, sin_signed, is_first_half):
    partner = jnp.where(is_first_half,
                        pltpu.roll(x, LANES - HEAD_DIM // 2, axis=1),
                        pltpu.roll(x, HEAD_DIM // 2, axis=1))
    return x * cos + partner * sin_signed


def _rope(x, cos, sin_signed, is_first_half):
    slabs = [_rope_slab(x[:, s:s + LANES], cos, sin_signed, is_first_half)
             for s in range(0, x.shape[1], LANES)]
    return jnp.concatenate(slabs, axis=1)


def _rope_t_store(xt, cos_t, sin_t, scale, out_ref):
    half = HEAD_DIM // 2
    for r0 in range(0, xt.shape[0], HEAD_DIM):
        x1 = xt[r0:r0 + half, :]
        x2 = xt[r0 + half:r0 + HEAD_DIM, :]
        out_ref[0, r0:r0 + half, :] = ((x1 * cos_t - x2 * sin_t) * scale).astype(out_ref.dtype)
        out_ref[0, r0 + half:r0 + HEAD_DIM, :] = (
            (x2 * cos_t + x1 * sin_t) * scale).astype(out_ref.dtype)


def _in_proj_kernel(h_ref, wn_ref, wt_ref, cw_ref, cb_ref, lg_ref, lb_ref,
                    cos_ref, sin_ref, cos_t_ref, sin_t_ref,
                    yc_ref, k_ref, ki2_ref, ga_ref, qt_ref, qit_ref, vt_ref, wt_out_ref,
                    ubuf_ref, ush_ref, *, tm, d_conv, d_attn, q_scale, idx_scale):
    t = pl.program_id(1)
    x = h_ref[0].astype(jnp.bfloat16)

    def proj(c0, width):
        return jnp.dot(x, wn_ref[:, c0:c0 + width], preferred_element_type=jnp.float32)

    def proj_t(r0, rows):
        return lax.dot_general(wt_ref[r0:r0 + rows, :], x, (((1,), (1,)), ((), ())),
                               preferred_element_type=jnp.float32)

    c0 = 0
    a = proj(c0, d_conv); c0 += d_conv
    g = proj(c0, d_conv); c0 += d_conv
    u = a * jax.nn.sigmoid(g)

    @pl.when(t == 0)
    def _():
        ubuf_ref[0:HALO, :] = jnp.zeros((HALO, d_conv), jnp.float32)

    ubuf_ref[HALO:HALO + tm, :] = u
    base = HALO - (CONV_WIDTH - 1)
    conv = jnp.broadcast_to(cb_ref[...], (tm, d_conv))
    for r in range(SUBLANES):
        taps = [j for j in range(CONV_WIDTH) if (base + j) % SUBLANES == r]
        if not taps:
            continue
        span = tm + ((base + taps[-1]) // SUBLANES) * SUBLANES
        ush_ref[0:span, :] = ubuf_ref[r:r + span, :]
        for j in taps:
            off = ((base + j) // SUBLANES) * SUBLANES
            conv = conv + cw_ref[j:j + 1, :] * ush_ref[off:off + tm, :]
    ubuf_ref[0:HALO, :] = u[tm - HALO:tm, :]

    zc = proj(c0, d_conv); c0 += d_conv
    yc = _silu(_layer_norm(conv, lg_ref[...], lb_ref[...])) * _silu(zc)
    yc_ref[0] = yc.astype(yc_ref.dtype)

    cos = cos_ref[...]
    sin = sin_ref[...]
    lane = lax.broadcasted_iota(jnp.int32, (tm, LANES), 1)
    first = (lane % HEAD_DIM) < (HEAD_DIM // 2)

    k = proj(c0, d_attn); c0 += d_attn
    k_ref[0] = _rope(k, cos, sin, first).astype(k_ref.dtype)
    za = proj(c0, d_attn); c0 += d_attn
    ga_ref[0] = _silu(za)
    ki2 = proj(c0, LANES)
    ki2_ref[0] = _rope_slab(ki2, cos, sin, first).astype(ki2_ref.dtype)

    cos_t = cos_t_ref[...]
    sin_t = sin_t_ref[...]
    r0 = 0
    _rope_t_store(proj_t(r0, d_attn), cos_t, sin_t, q_scale, qt_ref); r0 += d_attn
    _rope_t_store(proj_t(r0, d_attn), cos_t, sin_t, 1.0, qit_ref); r0 += d_attn
    vt_ref[0, 0] = proj_t(r0, d_attn).astype(vt_ref.dtype); r0 += d_attn
    wt_out_ref[0] = proj_t(r0, 2 * SUBLANES)[0:N_IDX_HEADS, :] * idx_scale


def _in_proj(h, wn, wt, cw, cb, lg, lb, cos, sin, cos_t, sin_t, *, q_scale, idx_scale):
    B, Lp, D = h.shape
    tm = SEQ_BLOCK
    d_conv = cw.shape[1]
    d_attn = N_HEADS * HEAD_DIM
    grid = (B, Lp // tm)
    row = lambda width: pl.BlockSpec((1, tm, width), lambda b, t: (b, t, 0))
    col = lambda rows: pl.BlockSpec((1, rows, tm), lambda b, t: (b, 0, t))
    full = lambda shape: pl.BlockSpec(shape, lambda b, t: (0,) * len(shape))
    bf = jnp.bfloat16
    out_shape = (
        jax.ShapeDtypeStruct((B, Lp, d_conv), bf),
        jax.ShapeDtypeStruct((B, Lp, d_attn), bf),
        jax.ShapeDtypeStruct((B, Lp, LANES), bf),
        jax.ShapeDtypeStruct((B, Lp, d_attn), jnp.float32),
        jax.ShapeDtypeStruct((B, d_attn, Lp), bf),
        jax.ShapeDtypeStruct((B, d_attn, Lp), bf),
        jax.ShapeDtypeStruct((B, Lp // tm, d_attn, tm), bf),
        jax.ShapeDtypeStruct((B, N_IDX_HEADS, Lp), jnp.float32),
    )
    out_specs = (row(d_conv), row(d_attn), row(LANES), row(d_attn), col(d_attn), col(d_attn),
                 pl.BlockSpec((1, 1, d_attn, tm), lambda b, t: (b, t, 0, 0)), col(N_IDX_HEADS))
    return pl.pallas_call(
        functools.partial(_in_proj_kernel, tm=tm, d_conv=d_conv, d_attn=d_attn,
                          q_scale=q_scale, idx_scale=idx_scale),
        grid=grid,
        in_specs=[row(D), full(wn.shape), full(wt.shape), full(cw.shape), full(cb.shape),
                  full(lg.shape), full(lb.shape),
                  pl.BlockSpec((tm, LANES), lambda b, t: (t, 0)),
                  pl.BlockSpec((tm, LANES), lambda b, t: (t, 0)),
                  pl.BlockSpec((HEAD_DIM // 2, tm), lambda b, t: (0, t)),
                  pl.BlockSpec((HEAD_DIM // 2, tm), lambda b, t: (0, t))],
        out_specs=out_specs,
        out_shape=out_shape,
        scratch_shapes=[pltpu.VMEM((HALO + tm, d_conv), jnp.float32),
                        pltpu.VMEM((HALO + tm, d_conv), jnp.float32)],
        compiler_params=pltpu.CompilerParams(
            dimension_semantics=("arbitrary", "arbitrary"), vmem_limit_bytes=VMEM_LIMIT),
        name="in_proj",
    )(h, wn, wt, cw, cb, lg, lb, cos, sin, cos_t, sin_t)


def _f32_to_key(x):
    bits = pltpu.bitcast(x, jnp.int32)
    return bits ^ ((bits >> 31) & np.int32(0x7FFFFFFF))


def _in_zero_class(key):
    return (key >= ZERO_CLASS_LO) & (key <= ZERO_CLASS_HI)


def _key_to_f32(key):
    val = pltpu.bitcast(key ^ ((key >> 31) & np.int32(0x7FFFFFFF)), jnp.float32)
    return jnp.where(_in_zero_class(key), 0.0, val)


def _key_succ(key):
    return jnp.where(_in_zero_class(key), ZERO_CLASS_HI + 1, key + 1)


def _fold_rows(x, op, rows):
    parts = [x[r:r + rows] for r in range(0, x.shape[0], rows)]
    while len(parts) > 1:
        nxt = [op(parts[a], parts[a + 1]) for a in range(0, len(parts) - 1, 2)]
        if len(parts) % 2:
            nxt.append(parts[-1])
        parts = nxt
    return parts[0]


def _dsa_kernel(qt_ref, qit_ref, wt_ref, k_ref, ki2_ref, vt_ref, ga_ref, o_ref,
                sc_ref, qm_ref, qim_ref, bias_ref, m_ref, l_ref, acc_ref, thr_ref, sta_ref, stb_ref,
                *, blk, topk):
    i = pl.program_id(1)
    n_chunks = i + 1
    kf = float(topk)

    srow = lax.broadcasted_iota(jnp.int32, (LANES, blk), 0)
    for h in range(N_HEADS):
        p, g = divmod(h, HEADS_PER_SLAB)
        own = (srow >= g * HEAD_DIM) & (srow < (g + 1) * HEAD_DIM)
        qm_ref[h] = jnp.where(own, qt_ref[0, p * LANES:(p + 1) * LANES, :].astype(jnp.float32),
                              0.0).astype(qm_ref.dtype)
        qim_ref[h] = jnp.where(own, qit_ref[0, p * LANES:(p + 1) * LANES, :].astype(jnp.float32),
                               0.0).astype(qim_ref.dtype)

    krow = lax.broadcasted_iota(jnp.int32, (blk, blk), 0)
    qcol = lax.broadcasted_iota(jnp.int32, (blk, blk), 1)

    def idx_logits(h, j):
        kic = ki2_ref[0, pl.ds(pl.multiple_of(j * blk, blk), blk), :]
        return jnp.dot(kic, qim_ref[h], preferred_element_type=jnp.float32)

    def idx_reduce(j, carry, lg_cur, lg_next, j_next):
        mn, mx = carry
        score = jnp.zeros((blk, blk), jnp.float32)
        for h in range(N_IDX_HEADS):
            if lg_next is not None:
                lg_next[h] = idx_logits(h, j_next)
            score = score + wt_ref[0, h:h + 1, :] * jnp.maximum(lg_cur[h], 0.0)
        causal = (krow + (j - i) * blk) <= qcol
        sc_ref[j] = jnp.where(causal, score, -jnp.inf)
        mn = jnp.minimum(mn, _fold_rows(jnp.where(causal, score, jnp.inf), jnp.minimum, SUBLANES))
        mx = jnp.maximum(mx, _fold_rows(jnp.where(causal, score, -jnp.inf), jnp.maximum, SUBLANES))
        return mn, mx

    last = n_chunks - 1
    n_pairs = n_chunks // 2
    odd = n_chunks % 2 == 1
    for h in range(N_IDX_HEADS):
        sta_ref[h] = idx_logits(h, 0)

    def idx_pair(pair, carry):
        j0 = 2 * pair
        carry = idx_reduce(j0, carry, sta_ref, stb_ref, j0 + 1)
        return idx_reduce(j0 + 1, carry, stb_ref, sta_ref, jnp.minimum(j0 + 2, last))

    extremes = lax.fori_loop(
        0, n_pairs, idx_pair,
        (jnp.full((SUBLANES, blk), jnp.inf, jnp.float32),
         jnp.full((SUBLANES, blk), -jnp.inf, jnp.float32)))
    mn8, mx8 = lax.cond(odd, lambda c: idx_reduce(last, c, sta_ref, None, None), lambda c: c,
                        extremes)
    smin = jnp.min(mn8, axis=0, keepdims=True)
    smax = jnp.max(mx8, axis=0, keepdims=True)

    def count_ge(thr_row):
        thr_b = jnp.broadcast_to(thr_row, (COUNT_ROWS, blk))

        def body(j, cnt):
            s = sc_ref[j]
            for r in range(0, blk, COUNT_ROWS):
                cnt = cnt + jnp.where(s[r:r + COUNT_ROWS] >= thr_b, 1.0, 0.0)
            return cnt

        cnt = lax.fori_loop(0, n_chunks, body, jnp.zeros((COUNT_ROWS, blk), jnp.float32))
        return jnp.sum(cnt, axis=0, keepdims=True)

    def bracket_extremes(lo_row, hi_row):
        lo_b = jnp.broadcast_to(lo_row, (SUBLANES, blk))
        hi_b = jnp.broadcast_to(hi_row, (SUBLANES, blk))

        def body(j, carry):
            a, b = carry
            s = sc_ref[j]
            for r in range(0, blk, SUBLANES):
                sr = s[r:r + SUBLANES]
                a = jnp.minimum(a, jnp.where(sr >= lo_b, sr, jnp.inf))
                b = jnp.maximum(b, jnp.where(sr < hi_b, sr, -jnp.inf))
            return a, b

        a, b = lax.fori_loop(0, n_chunks, body,
                             (jnp.full((SUBLANES, blk), jnp.inf, jnp.float32),
                              jnp.full((SUBLANES, blk), -jnp.inf, jnp.float32)))
        return jnp.min(a, axis=0, keepdims=True), jnp.max(b, axis=0, keepdims=True)

    n_valid = (i * blk + 1 + lax.broadcasted_iota(jnp.int32, (1, blk), 1)).astype(jnp.float32)
    active = n_valid > kf

    def count_step(state):
        it, lo, hi, c_lo, c_hi = state
        mid_val = _f32_to_key(0.5 * _key_to_f32(lo) + 0.5 * _key_to_f32(hi))
        mid_key = (lo >> 1) + (hi >> 1) + (lo & hi & 1)
        cand = jnp.where(it % SEARCH_PERIOD == KEY_STEP, mid_key, mid_val)
        cand = jnp.minimum(jnp.maximum(cand, lo + 1), hi - 1)
        c = count_ge(_key_to_f32(cand))
        ge = c >= kf
        return (it + 1, jnp.where(ge, cand, lo), jnp.where(ge, hi, cand),
                jnp.where(ge, c, c_lo), jnp.where(ge, c_hi, c))

    def tighten_step(state):
        it, lo, hi, c_lo, c_hi = state
        a, b = bracket_extremes(_key_to_f32(lo), _key_to_f32(hi))
        return (it + 1, jnp.maximum(lo, _f32_to_key(a)),
                jnp.minimum(hi, _key_succ(_f32_to_key(b))), c_lo, c_hi)

    def unresolved(state):
        _, lo, hi, c_lo, _ = state
        return jnp.where(active & (c_lo != kf) & (_key_succ(lo) < hi), 1.0, 0.0)

    def search_cond(state):
        pending = lax.cond(state[0] < MIN_SEARCH_STEPS, lambda: jnp.float32(1.0),
                           lambda: jnp.max(unresolved(state)))
        return (state[0] < MAX_SEARCH_STEPS) & (pending > 0.0)

    def search_step(state):
        return lax.cond(state[0] % SEARCH_PERIOD == TIGHTEN_STEP, tighten_step, count_step, state)

    def search_body(state):
        return search_step(search_step(state))

    lo0 = _f32_to_key(smin)
    hi0 = _key_succ(_f32_to_key(smax))
    state = lax.while_loop(search_cond, search_body,
                           (jnp.int32(0), lo0, hi0, n_valid, jnp.zeros((1, blk), jnp.float32)))
    _, lo, _, c_lo, _ = state
    thr_row = _key_to_f32(jnp.where(active, lo, KEY_NEG_INF + 1))
    thr_ref[...] = jnp.broadcast_to(thr_row, (SUBLANES, blk))
    excess = jnp.where(active & (c_lo > kf), 1.0, 0.0)

    @pl.when(jnp.max(excess) > 0.0)
    def _():
        thr_f = jnp.broadcast_to(thr_ref[0:1, :], (blk, blk))
        thr_c = jnp.broadcast_to(thr_ref[0:1, :], (COUNT_ROWS, blk))
        earlier = jnp.where(qcol < krow, 1.0, 0.0).astype(jnp.bfloat16)

        def gt_body(j, cnt):
            s = sc_ref[j]
            for r in range(0, blk, COUNT_ROWS):
                cnt = cnt + jnp.where(s[r:r + COUNT_ROWS] > thr_c, 1.0, 0.0)
            return cnt

        n_gt = jnp.sum(lax.fori_loop(0, n_chunks, gt_body,
                                     jnp.zeros((COUNT_ROWS, blk), jnp.float32)),
                       axis=0, keepdims=True)
        room = kf - n_gt

        def tie_body(j, seen):
            s = sc_ref[j]
            eq = s == thr_f
            eq_f = jnp.where(eq, 1.0, 0.0)
            prefix = jnp.dot(earlier, eq_f.astype(jnp.bfloat16),
                             preferred_element_type=jnp.float32)
            keep = (s > thr_f) | (eq & ((prefix + seen) < room))
            sc_ref[j] = jnp.where(keep, 1.0, -jnp.inf)
            return seen + jnp.sum(eq_f, axis=0, keepdims=True)

        lax.fori_loop(0, n_chunks, tie_body, jnp.zeros((1, blk), jnp.float32))
        thr_ref[...] = jnp.zeros((SUBLANES, blk), jnp.float32)

    m_ref[...] = jnp.full(m_ref.shape, NEG_BIG, jnp.float32)
    l_ref[...] = jnp.zeros(l_ref.shape, jnp.float32)
    acc_ref[...] = jnp.zeros(acc_ref.shape, jnp.float32)
    thr_full = jnp.broadcast_to(thr_ref[0:1, :], (blk, blk))

    def all_rows(x8, op):
        return jnp.broadcast_to(op(x8, axis=0, keepdims=True), x8.shape)

    def qk(h, j):
        p = h // HEADS_PER_SLAB
        kc = k_ref[0, pl.ds(pl.multiple_of(j * blk, blk), blk), p * LANES:(p + 1) * LANES]
        return jnp.dot(kc, qm_ref[h], preferred_element_type=jnp.float32)

    def attend(j, st_cur, st_next, j_next):
        bias_ref[...] = jnp.where(sc_ref[j] >= thr_full, 0.0, NEG_BIG)
        for h in range(N_HEADS):
            if st_next is not None:
                st_next[h] = qk(h, j_next)
            probs, alphas = [], []
            for c0 in range(0, blk, LANES):
                s = st_cur[h, :, c0:c0 + LANES] + bias_ref[:, c0:c0 + LANES]
                m_old = m_ref[h, :, c0:c0 + LANES]
                m_new = jnp.maximum(m_old, all_rows(_fold_rows(s, jnp.maximum, SUBLANES), jnp.max))
                alpha = jnp.exp2(m_old - m_new)
                pr = jnp.exp2(s - jnp.tile(m_new, (blk // SUBLANES, 1)))
                l_ref[h, :, c0:c0 + LANES] = (
                    alpha * l_ref[h, :, c0:c0 + LANES]
                    + all_rows(_fold_rows(pr, jnp.add, SUBLANES), jnp.sum))
                m_ref[h, :, c0:c0 + LANES] = m_new
                probs.append(pr.astype(jnp.bfloat16))
                alphas.append(alpha)
            o = jnp.dot(vt_ref[0, j, h * HEAD_DIM:(h + 1) * HEAD_DIM, :],
                        jnp.concatenate(probs, axis=1), preferred_element_type=jnp.float32)
            alpha_t = jnp.tile(jnp.concatenate(alphas, axis=1), (HEAD_DIM // SUBLANES, 1))
            acc_ref[h * HEAD_DIM:(h + 1) * HEAD_DIM, :] = (
                alpha_t * acc_ref[h * HEAD_DIM:(h + 1) * HEAD_DIM, :] + o)

    for h in range(N_HEADS):
        sta_ref[h] = qk(h, 0)

    def pair_body(pair, carry):
        j0 = 2 * pair
        attend(j0, sta_ref, stb_ref, j0 + 1)
        attend(j0 + 1, stb_ref, sta_ref, jnp.minimum(j0 + 2, last))
        return carry

    lax.fori_loop(0, n_pairs, pair_body, 0)

    @pl.when(odd)
    def _():
        attend(last, sta_ref, None, None)

    for h in range(N_HEADS):
        acc_ref[h * HEAD_DIM:(h + 1) * HEAD_DIM, :] = (
            acc_ref[h * HEAD_DIM:(h + 1) * HEAD_DIM, :]
            / jnp.tile(l_ref[h], (HEAD_DIM // SUBLANES, 1)))
    o_ref[0] = (acc_ref[...].T * ga_ref[0]).astype(o_ref.dtype)


def _dsa(qt, qit, wt, k, ki2, vt, ga, *, topk):
    B, Lp, d_attn = k.shape
    blk = SEQ_BLOCK
    grid = (B, Lp // blk)
    qcol = lambda rows: pl.BlockSpec((1, rows, blk), lambda b, i: (b, 0, i))
    seq = lambda width: pl.BlockSpec((1, Lp, width), lambda b, i: (b, 0, 0))
    qrow = lambda width: pl.BlockSpec((1, blk, width), lambda b, i: (b, i, 0))
    return pl.pallas_call(
        functools.partial(_dsa_kernel, blk=blk, topk=topk),
        grid=grid,
        in_specs=[qcol(d_attn), qcol(d_attn), qcol(N_IDX_HEADS), seq(d_attn), seq(LANES),
                  pl.BlockSpec((1, Lp // blk, d_attn, blk), lambda b, i: (b, 0, 0, 0)),
                  qrow(d_attn)],
        out_specs=qrow(d_attn),
        out_shape=jax.ShapeDtypeStruct((B, Lp, d_attn), jnp.bfloat16),
        scratch_shapes=[
            pltpu.VMEM((Lp // blk, blk, blk), jnp.float32),
            pltpu.VMEM((N_HEADS, LANES, blk), jnp.bfloat16),
            pltpu.VMEM((N_IDX_HEADS, LANES, blk), jnp.bfloat16),
            pltpu.VMEM((blk, blk), jnp.float32),
            pltpu.VMEM((N_HEADS, SUBLANES, blk), jnp.float32),
            pltpu.VMEM((N_HEADS, SUBLANES, blk), jnp.float32),
            pltpu.VMEM((d_attn, blk), jnp.float32),
            pltpu.VMEM((SUBLANES, blk), jnp.float32),
            pltpu.VMEM((N_HEADS, blk, blk), jnp.float32),
            pltpu.VMEM((N_HEADS, blk, blk), jnp.float32),
        ],
        compiler_params=pltpu.CompilerParams(
            dimension_semantics=("arbitrary", "arbitrary"), vmem_limit_bytes=VMEM_LIMIT),
        name="dsa",
    )(qt, qit, wt, k, ki2, vt, ga)


def _out_proj_kernel(yc_ref, ya_ref, h_ref, w_ref, g_ref, b_ref, o_ref, *, d_conv, alpha):
    y = jnp.dot(yc_ref[...], w_ref[0:d_conv, :], preferred_element_type=jnp.float32)
    y = y + jnp.dot(ya_ref[...], w_ref[d_conv:, :], preferred_element_type=jnp.float32)
    o_ref[...] = _layer_norm(alpha * h_ref[...] + y, g_ref[...], b_ref[...])


def _out_proj(yc, ya, h, w, g, b, *, tm, alpha):
    M, D = h.shape
    d_conv = yc.shape[1]
    row = lambda width: pl.BlockSpec((tm, width), lambda t: (t, 0))
    full = lambda shape: pl.BlockSpec(shape, lambda t: (0,) * len(shape))
    return pl.pallas_call(
        functools.partial(_out_proj_kernel, d_conv=d_conv, alpha=alpha),
        grid=(M // tm,),
        in_specs=[row(d_conv), row(ya.shape[1]), row(D), full(w.shape), full(g.shape),
                  full(b.shape)],
        out_specs=row(D),
        out_shape=jax.ShapeDtypeStruct((M, D), jnp.float32),
        compiler_params=pltpu.CompilerParams(
            dimension_semantics=("arbitrary",), vmem_limit_bytes=VMEM_LIMIT),
        name="out_proj",
    )(yc, ya, h, w, g, b)


def _row_tile(n, target, multiple):
    best = multiple
    for t in range(multiple, target + 1, multiple):
        if n % t == 0:
            best = t
    assert n % best == 0
    return best


def _rope_tables(length):
    inv_freq = ROPE_THETA ** (-jnp.arange(0, HEAD_DIM, 2, dtype=jnp.float32) / HEAD_DIM)
    ang = jnp.arange(length, dtype=jnp.float32)[:, None] * inv_freq[None, :]
    cos, sin = jnp.cos(ang), jnp.sin(ang)
    cos_l = jnp.tile(cos, (1, LANES // (HEAD_DIM // 2)))
    sin_l = jnp.tile(jnp.concatenate([-sin, sin], axis=1), (1, LANES // HEAD_DIM))
    return cos_l, sin_l, cos.T, sin.T


def kernel(x, meta_tokens, w_in, conv_w, conv_b, conv_ln_g, conv_ln_b, w_out,
           post_ln_g, post_ln_b):
    B, S, D = x.shape
    depth = w_in.shape[0]
    d_conv = conv_w.shape[2]
    d_attn = N_HEADS * HEAD_DIM
    L = S + N_META
    topk = min(TOPK_MAX, L // 4)
    Lp = -(-L // SEQ_BLOCK) * SEQ_BLOCK
    alpha = (2.0 * depth) ** 0.25
    q_scale = (HEAD_DIM ** -0.5) * LOG2_E
    idx_scale = (IDX_DIM ** -0.5) * (N_IDX_HEADS ** -0.5)

    meta = jnp.broadcast_to(meta_tokens[None].astype(x.dtype), (B, N_META, D))
    h = jnp.concatenate([meta, x, jnp.zeros((B, Lp - L, D), x.dtype)], axis=1)
    cos, sin, cos_t, sin_t = _rope_tables(Lp)

    bounds = np.cumsum([0, d_conv, d_conv, d_conv, d_attn, d_attn, d_attn, d_attn,
                        N_IDX_HEADS * IDX_DIM, IDX_DIM, N_IDX_HEADS])
    part = lambda n: w_in[:, :, bounds[n]:bounds[n + 1]]
    bf = jnp.bfloat16
    w_nat = jnp.concatenate(
        [part(0), part(1), part(2), part(4), part(6)] + [part(8)] * (LANES // IDX_DIM),
        axis=2).astype(bf)
    w_tr = jnp.swapaxes(jnp.concatenate(
        [part(3), part(7), part(5), part(9),
         jnp.zeros((depth, D, 2 * SUBLANES - N_IDX_HEADS), w_in.dtype)], axis=2), 1, 2).astype(bf)
    w_out_b = w_out.astype(bf)

    tm_out = _row_tile(B * Lp, 1024, 16)

    for l in range(depth):
        yc, k, ki2, ga, qt, qit, vt, wt = _in_proj(
            h, w_nat[l], w_tr[l], conv_w[l], conv_b[l][None], conv_ln_g[l][None],
            conv_ln_b[l][None], cos, sin, cos_t, sin_t, q_scale=q_scale, idx_scale=idx_scale)
        ya = _dsa(qt, qit, wt, k, ki2, vt, ga, topk=topk)
        h = _out_proj(yc.reshape(B * Lp, d_conv), ya.reshape(B * Lp, d_attn),
                      h.reshape(B * Lp, D), w_out_b[l], post_ln_g[l][None], post_ln_b[l][None],
                      tm=tm_out, alpha=alpha).reshape(B, Lp, D)

    return h[:, N_META:L]
```

```python
import functools

import jax
import jax.numpy as jnp
import numpy as np
from jax import lax
from jax.experimental import pallas as pl
from jax.experimental.pallas import tpu as pltpu

N_META = 16
CONV_WIDTH = 31
N_HEADS = 8
HEAD_DIM = 64
N_IDX_HEADS = 8
IDX_DIM = 64
TOPK_MAX = 256
ROPE_THETA = 10000.0
LN_EPS = 1e-5

LANES = 128
SUBLANES = 8
HEADS_PER_SLAB = LANES // HEAD_DIM
VT_ROWS = HEAD_DIM + 16
SEQ_BLOCK = 256
HALO = 32
COUNT_ROWS = 32
VMEM_LIMIT = 56 * 1024 * 1024

NEG_BIG = -1e30
KEY_NEG_INF = np.int32(-2139095041)
ZERO_CLASS_LO = np.int32(-0x00800000)
ZERO_CLASS_HI = np.int32(0x007FFFFF)
LOG2_E = 1.4426950408889634

SEARCH_PERIOD = 8
KEY_STEP = 3
TIGHTEN_STEP = 7
MAX_SEARCH_STEPS = 8 * 34
MIN_SEARCH_STEPS = 16


def _layer_norm(x, g, b):
    mu = jnp.mean(x, axis=-1, keepdims=True)
    xc = x - mu
    var = jnp.mean(xc * xc, axis=-1, keepdims=True)
    return xc * lax.rsqrt(var + LN_EPS) * g + b


def _silu(x):
    return x * jax.nn.sigmoid(x)


def _rope_slab(x, cos, sin_signed, is_first_half):
    partner = jnp.where(is_first_half,
                        pltpu.roll(x, LANES - HEAD_DIM // 2, axis=1),
                        pltpu.roll(x, HEAD_DIM // 2, axis=1))
    return x * cos + partner * sin_signed


def _rope(x, cos, sin_signed, is_first_half):
    slabs = [_rope_slab(x[:, s:s + LANES], cos, sin_signed, is_first_half)
             for s in range(0, x.shape[1], LANES)]
    return jnp.concatenate(slabs, axis=1)


def _rope_t_store(xt, cos_t, sin_t, scale, out_ref):
    half = HEAD_DIM // 2
    for r0 in range(0, xt.shape[0], HEAD_DIM):
        x1 = xt[r0:r0 + half, :]
        x2 = xt[r0 + half:r0 + HEAD_DIM, :]
        out_ref[0, r0:r0 + half, :] = ((x1 * cos_t - x2 * sin_t) * scale).astype(out_ref.dtype)
        out_ref[0, r0 + half:r0 + HEAD_DIM, :] = (
            (x2 * cos_t + x1 * sin_t) * scale).astype(out_ref.dtype)


def _in_proj_kernel(h_ref, wn_ref, wt_ref, cw_ref, cb_ref, lg_ref, lb_ref,
                    cos_ref, sin_ref, cos_t_ref, sin_t_ref,
                    yc_ref, k_ref, ki2_ref, ga_ref, qt_ref, qit_ref, vt_ref, wt_out_ref,
                    ubuf_ref, ush_ref, *, tm, d_conv, d_attn, q_scale, idx_scale):
    t = pl.program_id(1)
    x = h_ref[0].astype(jnp.bfloat16)

    def proj(c0, width):
        return jnp.dot(x, wn_ref[:, c0:c0 + width], preferred_element_type=jnp.float32)

    def proj_t(r0, rows):
        return lax.dot_general(wt_ref[r0:r0 + rows, :], x, (((1,), (1,)), ((), ())),
                               preferred_element_type=jnp.float32)

    c0 = 0
    a = proj(c0, d_conv); c0 += d_conv
    g = proj(c0, d_conv); c0 += d_conv
    u = a * jax.nn.sigmoid(g)

    @pl.when(t == 0)
    def _():
        ubuf_ref[0:HALO, :] = jnp.zeros((HALO, d_conv), jnp.float32)

    ubuf_ref[HALO:HALO + tm, :] = u
    base = HALO - (CONV_WIDTH - 1)
    conv = jnp.broadcast_to(cb_ref[...], (tm, d_conv))
    for r in range(SUBLANES):
        taps = [j for j in range(CONV_WIDTH) if (base + j) % SUBLANES == r]
        if not taps:
            continue
        span = tm + ((base + taps[-1]) // SUBLANES) * SUBLANES
        ush_ref[0:span, :] = ubuf_ref[r:r + span, :]
        for j in taps:
            off = ((base + j) // SUBLANES) * SUBLANES
            conv = conv + cw_ref[j:j + 1, :] * ush_ref[off:off + tm, :]
    ubuf_ref[0:HALO, :] = u[tm - HALO:tm, :]

    zc = proj(c0, d_conv); c0 += d_conv
    yc = _silu(_layer_norm(conv, lg_ref[...], lb_ref[...])) * _silu(zc)
    yc_ref[0] = yc.astype(yc_ref.dtype)

    cos = cos_ref[...]
    sin = sin_ref[...]
    lane = lax.broadcasted_iota(jnp.int32, (tm, LANES), 1)
    first = (lane % HEAD_DIM) < (HEAD_DIM // 2)

    k = proj(c0, d_attn); c0 += d_attn
    k_ref[0] = _rope(k, cos, sin, first).astype(k_ref.dtype)
    za = proj(c0, d_attn); c0 += d_attn
    ga_ref[0] = _silu(za)
    ki2 = proj(c0, LANES)
    ki2_ref[0] = _rope_slab(ki2, cos, sin, first).astype(ki2_ref.dtype)

    cos_t = cos_t_ref[...]
    sin_t = sin_t_ref[...]
    r0 = 0
    _rope_t_store(proj_t(r0, d_attn), cos_t, sin_t, q_scale, qt_ref); r0 += d_attn
    _rope_t_store(proj_t(r0, d_attn), cos_t, sin_t, 1.0, qit_ref); r0 += d_attn
    vt = proj_t(r0, d_attn).astype(vt_ref.dtype); r0 += d_attn
    for hd in range(N_HEADS):
        vt_ref[0, 0, hd, 0:HEAD_DIM, :] = vt[hd * HEAD_DIM:(hd + 1) * HEAD_DIM, :]
        vt_ref[0, 0, hd, HEAD_DIM:VT_ROWS, :] = jnp.ones((VT_ROWS - HEAD_DIM, tm), vt_ref.dtype)
    wt_out_ref[0] = proj_t(r0, 2 * SUBLANES)[0:N_IDX_HEADS, :] * idx_scale


def _in_proj(h, wn, wt, cw, cb, lg, lb, cos, sin, cos_t, sin_t, *, q_scale, idx_scale):
    B, Lp, D = h.shape
    tm = SEQ_BLOCK
    d_conv = cw.shape[1]
    d_attn = N_HEADS * HEAD_DIM
    grid = (B, Lp // tm)
    row = lambda width: pl.BlockSpec((1, tm, width), lambda b, t: (b, t, 0))
    col = lambda rows: pl.BlockSpec((1, rows, tm), lambda b, t: (b, 0, t))
    full = lambda shape: pl.BlockSpec(shape, lambda b, t: (0,) * len(shape))
    bf = jnp.bfloat16
    out_shape = (
        jax.ShapeDtypeStruct((B, Lp, d_conv), bf),
        jax.ShapeDtypeStruct((B, Lp, d_attn), bf),
        jax.ShapeDtypeStruct((B, Lp, LANES), bf),
        jax.ShapeDtypeStruct((B, Lp, d_attn), jnp.float32),
        jax.ShapeDtypeStruct((B, d_attn, Lp), bf),
        jax.ShapeDtypeStruct((B, d_attn, Lp), bf),
        jax.ShapeDtypeStruct((B, Lp // tm, N_HEADS, VT_ROWS, tm), bf),
        jax.ShapeDtypeStruct((B, N_IDX_HEADS, Lp), jnp.float32),
    )
    out_specs = (row(d_conv), row(d_attn), row(LANES), row(d_attn), col(d_attn), col(d_attn),
                 pl.BlockSpec((1, 1, N_HEADS, VT_ROWS, tm), lambda b, t: (b, t, 0, 0, 0)),
                 col(N_IDX_HEADS))
    return pl.pallas_call(
        functools.partial(_in_proj_kernel, tm=tm, d_conv=d_conv, d_attn=d_attn,
                          q_scale=q_scale, idx_scale=idx_scale),
        grid=grid,
        in_specs=[row(D), full(wn.shape), full(wt.shape), full(cw.shape), full(cb.shape),
                  full(lg.shape), full(lb.shape),
                  pl.BlockSpec((tm, LANES), lambda b, t: (t, 0)),
                  pl.BlockSpec((tm, LANES), lambda b, t: (t, 0)),
                  pl.BlockSpec((HEAD_DIM // 2, tm), lambda b, t: (0, t)),
                  pl.BlockSpec((HEAD_DIM // 2, tm), lambda b, t: (0, t))],
        out_specs=out_specs,
        out_shape=out_shape,
        scratch_shapes=[pltpu.VMEM((HALO + tm, d_conv), jnp.float32),
                        pltpu.VMEM((HALO + tm, d_conv), jnp.float32)],
        compiler_params=pltpu.CompilerParams(
            dimension_semantics=("arbitrary", "arbitrary"), vmem_limit_bytes=VMEM_LIMIT),
        name="in_proj",
    )(h, wn, wt, cw, cb, lg, lb, cos, sin, cos_t, sin_t)


def _f32_to_key(x):
    bits = pltpu.bitcast(x, jnp.int32)
    return bits ^ ((bits >> 31) & np.int32(0x7FFFFFFF))


def _in_zero_class(key):
    return (key >= ZERO_CLASS_LO) & (key <= ZERO_CLASS_HI)


def _key_to_f32(key):
    val = pltpu.bitcast(key ^ ((key >> 31) & np.int32(0x7FFFFFFF)), jnp.float32)
    return jnp.where(_in_zero_class(key), 0.0, val)


def _key_succ(key):
    return jnp.where(_in_zero_class(key), ZERO_CLASS_HI + 1, key + 1)


def _fold_rows(x, op, rows):
    parts = [x[r:r + rows] for r in range(0, x.shape[0], rows)]
    while len(parts) > 1:
        nxt = [op(parts[a], parts[a + 1]) for a in range(0, len(parts) - 1, 2)]
        if len(parts) % 2:
            nxt.append(parts[-1])
        parts = nxt
    return parts[0]


def _dsa_kernel(qt_ref, qit_ref, wt_ref, k_ref, ki2_ref, vt_ref, ga_ref, o_ref,
                sc_ref, qm_ref, qim_ref, bias_ref, m_ref, l_ref, acc_ref, thr_ref, sta_ref, stb_ref,
                *, blk, topk):
    i = pl.program_id(1)
    n_chunks = i + 1
    kf = float(topk)

    srow = lax.broadcasted_iota(jnp.int32, (LANES, blk), 0)
    for h in range(N_HEADS):
        p, g = divmod(h, HEADS_PER_SLAB)
        own = (srow >= g * HEAD_DIM) & (srow < (g + 1) * HEAD_DIM)
        qm_ref[h] = jnp.where(own, qt_ref[0, p * LANES:(p + 1) * LANES, :].astype(jnp.float32),
                              0.0).astype(qm_ref.dtype)
        qim_ref[h] = jnp.where(own, qit_ref[0, p * LANES:(p + 1) * LANES, :].astype(jnp.float32),
                               0.0).astype(qim_ref.dtype)

    krow = lax.broadcasted_iota(jnp.int32, (blk, blk), 0)
    qcol = lax.broadcasted_iota(jnp.int32, (blk, blk), 1)

    def idx_logits(h, j):
        kic = ki2_ref[0, pl.ds(pl.multiple_of(j * blk, blk), blk), :]
        return jnp.dot(kic, qim_ref[h], preferred_element_type=jnp.float32)

    def idx_reduce(j, carry, lg_cur, lg_next, j_next):
        mn, mx = carry
        score = jnp.zeros((blk, blk), jnp.float32)
        for h in range(N_IDX_HEADS):
            if lg_next is not None:
                lg_next[h] = idx_logits(h, j_next)
            score = score + wt_ref[0, h:h + 1, :] * jnp.maximum(lg_cur[h], 0.0)
        causal = (krow + (j - i) * blk) <= qcol
        sc_ref[j] = jnp.where(causal, score, -jnp.inf)
        mn = jnp.minimum(mn, _fold_rows(jnp.where(causal, score, jnp.inf), jnp.minimum, SUBLANES))
        mx = jnp.maximum(mx, _fold_rows(jnp.where(causal, score, -jnp.inf), jnp.maximum, SUBLANES))
        return mn, mx

    last = n_chunks - 1
    n_pairs = n_chunks // 2
    odd = n_chunks % 2 == 1
    for h in range(N_IDX_HEADS):
        sta_ref[h] = idx_logits(h, 0)

    def idx_pair(pair, carry):
        j0 = 2 * pair
        carry = idx_reduce(j0, carry, sta_ref, stb_ref, j0 + 1)
        return idx_reduce(j0 + 1, carry, stb_ref, sta_ref, jnp.minimum(j0 + 2, last))

    extremes = lax.fori_loop(
        0, n_pairs, idx_pair,
        (jnp.full((SUBLANES, blk), jnp.inf, jnp.float32),
         jnp.full((SUBLANES, blk), -jnp.inf, jnp.float32)))
    mn8, mx8 = lax.cond(odd, lambda c: idx_reduce(last, c, sta_ref, None, None), lambda c: c,
                        extremes)
    smin = jnp.min(mn8, axis=0, keepdims=True)
    smax = jnp.max(mx8, axis=0, keepdims=True)

    def count_ge(thr_row):
        thr_b = jnp.broadcast_to(thr_row, (COUNT_ROWS, blk))

        def body(j, cnt):
            s = sc_ref[j]
            for r in range(0, blk, COUNT_ROWS):
                cnt = cnt + jnp.where(s[r:r + COUNT_ROWS] >= thr_b, 1.0, 0.0)
            return cnt

        cnt = lax.fori_loop(0, n_chunks, body, jnp.zeros((COUNT_ROWS, blk), jnp.float32))
        return jnp.sum(cnt, axis=0, keepdims=True)

    def bracket_extremes(lo_row, hi_row):
        lo_b = jnp.broadcast_to(lo_row, (SUBLANES, blk))
        hi_b = jnp.broadcast_to(hi_row, (SUBLANES, blk))

        def body(j, carry):
            a, b = carry
            s = sc_ref[j]
            for r in range(0, blk, SUBLANES):
                sr = s[r:r + SUBLANES]
                a = jnp.minimum(a, jnp.where(sr >= lo_b, sr, jnp.inf))
                b = jnp.maximum(b, jnp.where(sr < hi_b, sr, -jnp.inf))
            return a, b

        a, b = lax.fori_loop(0, n_chunks, body,
                             (jnp.full((SUBLANES, blk), jnp.inf, jnp.float32),
                              jnp.full((SUBLANES, blk), -jnp.inf, jnp.float32)))
        return jnp.min(a, axis=0, keepdims=True), jnp.max(b, axis=0, keepdims=True)

    n_valid = (i * blk + 1 + lax.broadcasted_iota(jnp.int32, (1, blk), 1)).astype(jnp.float32)
    active = n_valid > kf

    def count_step(state):
        it, lo, hi, c_lo, c_hi = state
        mid_val = _f32_to_key(0.5 * _key_to_f32(lo) + 0.5 * _key_to_f32(hi))
        mid_key = (lo >> 1) + (hi >> 1) + (lo & hi & 1)
        cand = jnp.where(it % SEARCH_PERIOD == KEY_STEP, mid_key, mid_val)
        cand = jnp.minimum(jnp.maximum(cand, lo + 1), hi - 1)
        c = count_ge(_key_to_f32(cand))
        ge = c >= kf
        return (it + 1, jnp.where(ge, cand, lo), jnp.where(ge, hi, cand),
                jnp.where(ge, c, c_lo), jnp.where(ge, c_hi, c))

    def tighten_step(state):
        it, lo, hi, c_lo, c_hi = state
        a, b = bracket_extremes(_key_to_f32(lo), _key_to_f32(hi))
        return (it + 1, jnp.maximum(lo, _f32_to_key(a)),
                jnp.minimum(hi, _key_succ(_f32_to_key(b))), c_lo, c_hi)

    def unresolved(state):
        _, lo, hi, c_lo, _ = state
        return jnp.where(active & (c_lo != kf) & (_key_succ(lo) < hi), 1.0, 0.0)

    def search_cond(state):
        pending = lax.cond(state[0] < MIN_SEARCH_STEPS, lambda: jnp.float32(1.0),
                           lambda: jnp.max(unresolved(state)))
        return (state[0] < MAX_SEARCH_STEPS) & (pending > 0.0)

    def search_step(state):
        return lax.cond(state[0] % SEARCH_PERIOD == TIGHTEN_STEP, tighten_step, count_step, state)

    def search_body(state):
        return search_step(search_step(state))

    lo0 = _f32_to_key(smin)
    hi0 = _key_succ(_f32_to_key(smax))
    state = lax.while_loop(search_cond, search_body,
                           (jnp.int32(0), lo0, hi0, n_valid, jnp.zeros((1, blk), jnp.float32)))
    _, lo, _, c_lo, _ = state
    thr_row = _key_to_f32(jnp.where(active, lo, KEY_NEG_INF + 1))
    thr_ref[...] = jnp.broadcast_to(thr_row, (SUBLANES, blk))
    excess = jnp.where(active & (c_lo > kf), 1.0, 0.0)

    @pl.when(jnp.max(excess) > 0.0)
    def _():
        thr_f = jnp.broadcast_to(thr_ref[0:1, :], (blk, blk))
        thr_c = jnp.broadcast_to(thr_ref[0:1, :], (COUNT_ROWS, blk))
        earlier = jnp.where(qcol < krow, 1.0, 0.0).astype(jnp.bfloat16)

        def gt_body(j, cnt):
            s = sc_ref[j]
            for r in range(0, blk, COUNT_ROWS):
                cnt = cnt + jnp.where(s[r:r + COUNT_ROWS] > thr_c, 1.0, 0.0)
            return cnt

        n_gt = jnp.sum(lax.fori_loop(0, n_chunks, gt_body,
                                     jnp.zeros((COUNT_ROWS, blk), jnp.float32)),
                       axis=0, keepdims=True)
        room = kf - n_gt

        def tie_body(j, seen):
            s = sc_ref[j]
            eq = s == thr_f
            eq_f = jnp.where(eq, 1.0, 0.0)
            prefix = jnp.dot(earlier, eq_f.astype(jnp.bfloat16),
                             preferred_element_type=jnp.float32)
            keep = (s > thr_f) | (eq & ((prefix + seen) < room))
            sc_ref[j] = jnp.where(keep, 1.0, -jnp.inf)
            return seen + jnp.sum(eq_f, axis=0, keepdims=True)

        lax.fori_loop(0, n_chunks, tie_body, jnp.zeros((1, blk), jnp.float32))
        thr_ref[...] = jnp.zeros((SUBLANES, blk), jnp.float32)

    m_ref[...] = jnp.full(m_ref.shape, NEG_BIG, jnp.float32)
    l_ref[...] = jnp.zeros(l_ref.shape, jnp.float32)
    acc_ref[...] = jnp.zeros(acc_ref.shape, jnp.float32)
    thr_full = jnp.broadcast_to(thr_ref[0:1, :], (blk, blk))

    def all_rows(x8, op):
        return jnp.broadcast_to(op(x8, axis=0, keepdims=True), x8.shape)

    def qk(h, j):
        p = h // HEADS_PER_SLAB
        kc = k_ref[0, pl.ds(pl.multiple_of(j * blk, blk), blk), p * LANES:(p + 1) * LANES]
        return jnp.dot(kc, qm_ref[h], preferred_element_type=jnp.float32)

    def set_mask(j):
        bias_ref[...] = jnp.where(sc_ref[j] >= thr_full, 0.0, NEG_BIG)

    def attend(j, st_cur, st_next, j_next):
        if st_next is not None:
            set_mask(j_next)
        for h in range(N_HEADS):
            if st_next is not None:
                st_next[h] = bias_ref[...] + qk(h, j_next)
            probs, alphas = [], []
            for c0 in range(0, blk, LANES):
                s = st_cur[h, :, c0:c0 + LANES]
                m_old = m_ref[h, :, c0:c0 + LANES]
                m_new = jnp.maximum(m_old, all_rows(_fold_rows(s, jnp.maximum, SUBLANES), jnp.max))
                alphas.append(jnp.exp2(m_old - m_new))
                probs.append(jnp.exp2(s - jnp.tile(m_new, (blk // SUBLANES, 1))).astype(jnp.bfloat16))
                m_ref[h, :, c0:c0 + LANES] = m_new
            o = jnp.dot(vt_ref[0, j, h], jnp.concatenate(probs, axis=1),
                        preferred_element_type=jnp.float32)
            alpha = jnp.concatenate(alphas, axis=1)
            l_ref[h] = alpha * l_ref[h] + o[HEAD_DIM:HEAD_DIM + SUBLANES, :]
            acc_ref[h * HEAD_DIM:(h + 1) * HEAD_DIM, :] = (
                jnp.tile(alpha, (HEAD_DIM // SUBLANES, 1)) * acc_ref[h * HEAD_DIM:(h + 1) * HEAD_DIM, :]
                + o[0:HEAD_DIM, :])

    set_mask(0)
    for h in range(N_HEADS):
        sta_ref[h] = bias_ref[...] + qk(h, 0)

    def pair_body(pair, carry):
        j0 = 2 * pair
        attend(j0, sta_ref, stb_ref, j0 + 1)
        attend(j0 + 1, stb_ref, sta_ref, jnp.minimum(j0 + 2, last))
        return carry

    lax.fori_loop(0, n_pairs, pair_body, 0)

    @pl.when(odd)
    def _():
        attend(last, sta_ref, None, None)

    for h in range(N_HEADS):
        acc_ref[h * HEAD_DIM:(h + 1) * HEAD_DIM, :] = (
            acc_ref[h * HEAD_DIM:(h + 1) * HEAD_DIM, :]
            / jnp.tile(l_ref[h], (HEAD_DIM // SUBLANES, 1)))
    o_ref[0] = (acc_ref[...].T * ga_ref[0]).astype(o_ref.dtype)


def _dsa(qt, qit, wt, k, ki2, vt, ga, *, topk):
    B, Lp, d_attn = k.shape
    blk = SEQ_BLOCK
    grid = (B, Lp // blk)
    qcol = lambda rows: pl.BlockSpec((1, rows, blk), lambda b, i: (b, 0, i))
    seq = lambda width: pl.BlockSpec((1, Lp, width), lambda b, i: (b, 0, 0))
    qrow = lambda width: pl.BlockSpec((1, blk, width), lambda b, i: (b, i, 0))
    return pl.pallas_call(
        functools.partial(_dsa_kernel, blk=blk, topk=topk),
        grid=grid,
        in_specs=[qcol(d_attn), qcol(d_attn), qcol(N_IDX_HEADS), seq(d_attn), seq(LANES),
                  pl.BlockSpec((1, Lp // blk, N_HEADS, VT_ROWS, blk), lambda b, i: (b, 0, 0, 0, 0)),
                  qrow(d_attn)],
        out_specs=qrow(d_attn),
        out_shape=jax.ShapeDtypeStruct((B, Lp, d_attn), jnp.bfloat16),
        scratch_shapes=[
            pltpu.VMEM((Lp // blk, blk, blk), jnp.float32),
            pltpu.VMEM((N_HEADS, LANES, blk), jnp.bfloat16),
            pltpu.VMEM((N_IDX_HEADS, LANES, blk), jnp.bfloat16),
            pltpu.VMEM((blk, blk), jnp.float32),
            pltpu.VMEM((N_HEADS, SUBLANES, blk), jnp.float32),
            pltpu.VMEM((N_HEADS, SUBLANES, blk), jnp.float32),
            pltpu.VMEM((d_attn, blk), jnp.float32),
            pltpu.VMEM((SUBLANES, blk), jnp.float32),
            pltpu.VMEM((N_HEADS, blk, blk), jnp.float32),
            pltpu.VMEM((N_HEADS, blk, blk), jnp.float32),
        ],
        compiler_params=pltpu.CompilerParams(
            dimension_semantics=("arbitrary", "arbitrary"), vmem_limit_bytes=VMEM_LIMIT),
        name="dsa",
    )(qt, qit, wt, k, ki2, vt, ga)


def _out_proj_kernel(yc_ref, ya_ref, h_ref, w_ref, g_ref, b_ref, o_ref, *, d_conv, alpha):
    y = jnp.dot(yc_ref[...], w_ref[0:d_conv, :], preferred_element_type=jnp.float32)
    y = y + jnp.dot(ya_ref[...], w_ref[d_conv:, :], preferred_element_type=jnp.float32)
    o_ref[...] = _layer_norm(alpha * h_ref[...] + y, g_ref[...], b_ref[...])


def _out_proj(yc, ya, h, w, g, b, *, tm, alpha):
    M, D = h.shape
    d_conv = yc.shape[1]
    row = lambda width: pl.BlockSpec((tm, width), lambda t: (t, 0))
    full = lambda shape: pl.BlockSpec(shape, lambda t: (0,) * len(shape))
    return pl.pallas_call(
        functools.partial(_out_proj_kernel, d_conv=d_conv, alpha=alpha),
        grid=(M // tm,),
        in_specs=[row(d_conv), row(ya.shape[1]), row(D), full(w.shape), full(g.shape),
                  full(b.shape)],
        out_specs=row(D),
        out_shape=jax.ShapeDtypeStruct((M, D), jnp.float32),
        compiler_params=pltpu.CompilerParams(
            dimension_semantics=("arbitrary",), vmem_limit_bytes=VMEM_LIMIT),
        name="out_proj",
    )(yc, ya, h, w, g, b)


def _row_tile(n, target, multiple):
    best = multiple
    for t in range(multiple, target + 1, multiple):
        if n % t == 0:
            best = t
    assert n % best == 0
    return best


def _rope_tables(length):
    inv_freq = ROPE_THETA ** (-jnp.arange(0, HEAD_DIM, 2, dtype=jnp.float32) / HEAD_DIM)
    ang = jnp.arange(length, dtype=jnp.float32)[:, None] * inv_freq[None, :]
    cos, sin = jnp.cos(ang), jnp.sin(ang)
    cos_l = jnp.tile(cos, (1, LANES // (HEAD_DIM // 2)))
    sin_l = jnp.tile(jnp.concatenate([-sin, sin], axis=1), (1, LANES // HEAD_DIM))
    return cos_l, sin_l, cos.T, sin.T


def kernel(x, meta_tokens, w_in, conv_w, conv_b, conv_ln_g, conv_ln_b, w_out,
           post_ln_g, post_ln_b):
    B, S, D = x.shape
    depth = w_in.shape[0]
    d_conv = conv_w.shape[2]
    d_attn = N_HEADS * HEAD_DIM
    L = S + N_META
    topk = min(TOPK_MAX, L // 4)
    Lp = -(-L // SEQ_BLOCK) * SEQ_BLOCK
    alpha = (2.0 * depth) ** 0.25
    q_scale = (HEAD_DIM ** -0.5) * LOG2_E
    idx_scale = (IDX_DIM ** -0.5) * (N_IDX_HEADS ** -0.5)

    meta = jnp.broadcast_to(meta_tokens[None].astype(x.dtype), (B, N_META, D))
    h = jnp.concatenate([meta, x, jnp.zeros((B, Lp - L, D), x.dtype)], axis=1)
    cos, sin, cos_t, sin_t = _rope_tables(Lp)

    bounds = np.cumsum([0, d_conv, d_conv, d_conv, d_attn, d_attn, d_attn, d_attn,
                        N_IDX_HEADS * IDX_DIM, IDX_DIM, N_IDX_HEADS])
    part = lambda n: w_in[:, :, bounds[n]:bounds[n + 1]]
    bf = jnp.bfloat16
    w_nat = jnp.concatenate(
        [part(0), part(1), part(2), part(4), part(6)] + [part(8)] * (LANES // IDX_DIM),
        axis=2).astype(bf)
    w_tr = jnp.swapaxes(jnp.concatenate(
        [part(3), part(7), part(5), part(9),
         jnp.zeros((depth, D, 2 * SUBLANES - N_IDX_HEADS), w_in.dtype)], axis=2), 1, 2).astype(bf)
    w_out_b = w_out.astype(bf)

    tm_out = _row_tile(B * Lp, 1024, 16)

    for l in range(depth):
        yc, k, ki2, ga, qt, qit, vt, wt = _in_proj(
            h, w_nat[l], w_tr[l], conv_w[l], conv_b[l][None], conv_ln_g[l][None],
            conv_ln_b[l][None], cos, sin, cos_t, sin_t, q_scale=q_scale, idx_scale=idx_scale)
        ya = _dsa(qt, qit, wt, k, ki2, vt, ga, topk=topk)
        h = _out_proj(yc.reshape(B * Lp, d_conv), ya.reshape(B * Lp, d_attn),
                      h.reshape(B * Lp, D), w_out_b[l], post_ln_g[l][None], post_ln_b[l][None],
                      tm=tm_out, alpha=alpha).reshape(B, Lp, D)

    return h[:, N_META:L]
```

```python
import functools

import jax
import jax.numpy as jnp
import numpy as np
from jax import lax
from jax.experimental import pallas as pl
from jax.experimental.pallas import tpu as pltpu

N_META = 16
CONV_WIDTH = 31
N_HEADS = 8
HEAD_DIM = 64
N_IDX_HEADS = 8
IDX_DIM = 64
TOPK_MAX = 256
ROPE_THETA = 10000.0
LN_EPS = 1e-5

LANES = 128
SUBLANES = 8
HEADS_PER_SLAB = LANES // HEAD_DIM
VT_ROWS = HEAD_DIM + 16
SEQ_BLOCK = 256
HALO = 32
COUNT_ROWS = 32
VMEM_LIMIT = 56 * 1024 * 1024

NEG_BIG = -1e30
KEY_NEG_INF = np.int32(-2139095041)
ZERO_CLASS_LO = np.int32(-0x00800000)
ZERO_CLASS_HI = np.int32(0x007FFFFF)
LOG2_E = 1.4426950408889634

SEARCH_PERIOD = 8
KEY_STEP = 3
TIGHTEN_STEP = 7
MAX_SEARCH_STEPS = 8 * 34
MIN_SEARCH_STEPS = 18


def _layer_norm(x, g, b):
    mu = jnp.mean(x, axis=-1, keepdims=True)
    xc = x - mu
    var = jnp.mean(xc * xc, axis=-1, keepdims=True)
    return xc * lax.rsqrt(var + LN_EPS) * g + b


def _silu(x):
    return x * jax.nn.sigmoid(x)


def _rope_slab(x, cos, sin_signed, is_first_half):
    partner = jnp.where(is_first_half,
                        pltpu.roll(x, LANES - HEAD_DIM // 2, axis=1),
                        pltpu.roll(x, HEAD_DIM // 2, axis=1))
    return x * cos + partner * sin_signed


def _rope(x, cos, sin_signed, is_first_half):
    slabs = [_rope_slab(x[:, s:s + LANES], cos, sin_signed, is_first_half)
             for s in range(0, x.shape[1], LANES)]
    return jnp.concatenate(slabs, axis=1)


def _rope_t_store(xt, cos_t, sin_t, scale, out_ref):
    half = HEAD_DIM // 2
    for r0 in range(0, xt.shape[0], HEAD_DIM):
        x1 = xt[r0:r0 + half, :]
        x2 = xt[r0 + half:r0 + HEAD_DIM, :]
        out_ref[0, r0:r0 + half, :] = ((x1 * cos_t - x2 * sin_t) * scale).astype(out_ref.dtype)
        out_ref[0, r0 + half:r0 + HEAD_DIM, :] = (
            (x2 * cos_t + x1 * sin_t) * scale).astype(out_ref.dtype)


def _in_proj_kernel(h_ref, wn_ref, wt_ref, cw_ref, cb_ref, lg_ref, lb_ref,
                    cos_ref, sin_ref, cos_t_ref, sin_t_ref,
                    yc_ref, k_ref, ki2_ref, ga_ref, qt_ref, qit_ref, vt_ref, wt_out_ref,
                    ubuf_ref, ush_ref, *, tm, d_conv, d_attn, q_scale, idx_scale):
    t = pl.program_id(1)
    x = h_ref[0].astype(jnp.bfloat16)

    def proj(c0, width):
        return jnp.dot(x, wn_ref[:, c0:c0 + width], preferred_element_type=jnp.float32)

    def proj_t(r0, rows):
        return lax.dot_general(wt_ref[r0:r0 + rows, :], x, (((1,), (1,)), ((), ())),
                               preferred_element_type=jnp.float32)

    c0 = 0
    a = proj(c0, d_conv); c0 += d_conv
    g = proj(c0, d_conv); c0 += d_conv
    u = a * jax.nn.sigmoid(g)

    @pl.when(t == 0)
    def _():
        ubuf_ref[0:HALO, :] = jnp.zeros((HALO, d_conv), jnp.float32)

    ubuf_ref[HALO:HALO + tm, :] = u
    base = HALO - (CONV_WIDTH - 1)
    conv = jnp.broadcast_to(cb_ref[...], (tm, d_conv))
    for r in range(SUBLANES):
        taps = [j for j in range(CONV_WIDTH) if (base + j) % SUBLANES == r]
        if not taps:
            continue
        span = tm + ((base + taps[-1]) // SUBLANES) * SUBLANES
        ush_ref[0:span, :] = ubuf_ref[r:r + span, :]
        for j in taps:
            off = ((base + j) // SUBLANES) * SUBLANES
            conv = conv + cw_ref[j:j + 1, :] * ush_ref[off:off + tm, :]
    ubuf_ref[0:HALO, :] = u[tm - HALO:tm, :]

    zc = proj(c0, d_conv); c0 += d_conv
    yc = _silu(_layer_norm(conv, lg_ref[...], lb_ref[...])) * _silu(zc)
    yc_ref[0] = yc.astype(yc_ref.dtype)

    cos = cos_ref[...]
    sin = sin_ref[...]
    lane = lax.broadcasted_iota(jnp.int32, (tm, LANES), 1)
    first = (lane % HEAD_DIM) < (HEAD_DIM // 2)

    k = proj(c0, d_attn); c0 += d_attn
    k_ref[0] = _rope(k, cos, sin, first).astype(k_ref.dtype)
    za = proj(c0, d_attn); c0 += d_attn
    ga_ref[0] = _silu(za).astype(ga_ref.dtype)
    ki2 = proj(c0, LANES)
    ki2_ref[0] = _rope_slab(ki2, cos, sin, first).astype(ki2_ref.dtype)

    cos_t = cos_t_ref[...]
    sin_t = sin_t_ref[...]
    r0 = 0
    _rope_t_store(proj_t(r0, d_attn), cos_t, sin_t, q_scale, qt_ref); r0 += d_attn
    _rope_t_store(proj_t(r0, d_attn), cos_t, sin_t, 1.0, qit_ref); r0 += d_attn
    vt = proj_t(r0, d_attn).astype(vt_ref.dtype); r0 += d_attn
    for hd in range(N_HEADS):
        vt_ref[0, 0, hd, 0:HEAD_DIM, :] = vt[hd * HEAD_DIM:(hd + 1) * HEAD_DIM, :]
        vt_ref[0, 0, hd, HEAD_DIM:VT_ROWS, :] = jnp.ones((VT_ROWS - HEAD_DIM, tm), vt_ref.dtype)
    wt_out_ref[0] = proj_t(r0, 2 * SUBLANES)[0:N_IDX_HEADS, :] * idx_scale


def _in_proj(h, wn, wt, cw, cb, lg, lb, cos, sin, cos_t, sin_t, *, q_scale, idx_scale):
    B, Lp, D = h.shape
    tm = SEQ_BLOCK
    d_conv = cw.shape[1]
    d_attn = N_HEADS * HEAD_DIM
    grid = (B, Lp // tm)
    row = lambda width: pl.BlockSpec((1, tm, width), lambda b, t: (b, t, 0))
    col = lambda rows: pl.BlockSpec((1, rows, tm), lambda b, t: (b, 0, t))
    full = lambda shape: pl.BlockSpec(shape, lambda b, t: (0,) * len(shape))
    bf = jnp.bfloat16
    out_shape = (
        jax.ShapeDtypeStruct((B, Lp, d_conv), bf),
        jax.ShapeDtypeStruct((B, Lp, d_attn), bf),
        jax.ShapeDtypeStruct((B, Lp, LANES), bf),
        jax.ShapeDtypeStruct((B, Lp, d_attn), bf),
        jax.ShapeDtypeStruct((B, d_attn, Lp), bf),
        jax.ShapeDtypeStruct((B, d_attn, Lp), bf),
        jax.ShapeDtypeStruct((B, Lp // tm, N_HEADS, VT_ROWS, tm), bf),
        jax.ShapeDtypeStruct((B, N_IDX_HEADS, Lp), jnp.float32),
    )
    out_specs = (row(d_conv), row(d_attn), row(LANES), row(d_attn), col(d_attn), col(d_attn),
                 pl.BlockSpec((1, 1, N_HEADS, VT_ROWS, tm), lambda b, t: (b, t, 0, 0, 0)),
                 col(N_IDX_HEADS))
    return pl.pallas_call(
        functools.partial(_in_proj_kernel, tm=tm, d_conv=d_conv, d_attn=d_attn,
                          q_scale=q_scale, idx_scale=idx_scale),
        grid=grid,
        in_specs=[row(D), full(wn.shape), full(wt.shape), full(cw.shape), full(cb.shape),
                  full(lg.shape), full(lb.shape),
                  pl.BlockSpec((tm, LANES), lambda b, t: (t, 0)),
                  pl.BlockSpec((tm, LANES), lambda b, t: (t, 0)),
                  pl.BlockSpec((HEAD_DIM // 2, tm), lambda b, t: (0, t)),
                  pl.BlockSpec((HEAD_DIM // 2, tm), lambda b, t: (0, t))],
        out_specs=out_specs,
        out_shape=out_shape,
        scratch_shapes=[pltpu.VMEM((HALO + tm, d_conv), jnp.float32),
                        pltpu.VMEM((HALO + tm, d_conv), jnp.float32)],
        compiler_params=pltpu.CompilerParams(
            dimension_semantics=("arbitrary", "arbitrary"), vmem_limit_bytes=VMEM_LIMIT),
        name="in_proj",
    )(h, wn, wt, cw, cb, lg, lb, cos, sin, cos_t, sin_t)


def _f32_to_key(x):
    bits = pltpu.bitcast(x, jnp.int32)
    return bits ^ ((bits >> 31) & np.int32(0x7FFFFFFF))


def _in_zero_class(key):
    return (key >= ZERO_CLASS_LO) & (key <= ZERO_CLASS_HI)


def _key_to_f32(key):
    val = pltpu.bitcast(key ^ ((key >> 31) & np.int32(0x7FFFFFFF)), jnp.float32)
    return jnp.where(_in_zero_class(key), 0.0, val)


def _key_succ(key):
    return jnp.where(_in_zero_class(key), ZERO_CLASS_HI + 1, key + 1)


def _fold_rows(x, op, rows):
    parts = [x[r:r + rows] for r in range(0, x.shape[0], rows)]
    while len(parts) > 1:
        nxt = [op(parts[a], parts[a + 1]) for a in range(0, len(parts) - 1, 2)]
        if len(parts) % 2:
            nxt.append(parts[-1])
        parts = nxt
    return parts[0]


def _dsa_kernel(qt_ref, qit_ref, wt_ref, k_ref, ki2_ref, vt_ref, ga_ref, o_ref,
                sc_ref, qm_ref, qim_ref, bias_ref, m_ref, l_ref, acc_ref, thr_ref, sta_ref, stb_ref,
                *, blk, topk):
    i = pl.program_id(1)
    n_chunks = i + 1
    kf = float(topk)

    srow = lax.broadcasted_iota(jnp.int32, (LANES, blk), 0)
    for h in range(N_HEADS):
        p, g = divmod(h, HEADS_PER_SLAB)
        own = (srow >= g * HEAD_DIM) & (srow < (g + 1) * HEAD_DIM)
        qm_ref[h] = jnp.where(own, qt_ref[0, p * LANES:(p + 1) * LANES, :].astype(jnp.float32),
                              0.0).astype(qm_ref.dtype)
        qim_ref[h] = jnp.where(own, qit_ref[0, p * LANES:(p + 1) * LANES, :].astype(jnp.float32),
                               0.0).astype(qim_ref.dtype)

    krow = lax.broadcasted_iota(jnp.int32, (blk, blk), 0)
    qcol = lax.broadcasted_iota(jnp.int32, (blk, blk), 1)

    def idx_logits(h, j):
        kic = ki2_ref[0, pl.ds(pl.multiple_of(j * blk, blk), blk), :]
        return jnp.dot(kic, qim_ref[h], preferred_element_type=jnp.float32)

    def idx_reduce(j, carry, lg_cur, lg_next, j_next):
        mn, mx = carry
        score = jnp.zeros((blk, blk), jnp.float32)
        for h in range(N_IDX_HEADS):
            if lg_next is not None:
                lg_next[h] = idx_logits(h, j_next)
            score = score + wt_ref[0, h:h + 1, :] * jnp.maximum(lg_cur[h], 0.0)
        causal = (krow + (j - i) * blk) <= qcol
        sc_ref[j] = jnp.where(causal, score, -jnp.inf)
        mn = jnp.minimum(mn, _fold_rows(jnp.where(causal, score, jnp.inf), jnp.minimum, SUBLANES))
        mx = jnp.maximum(mx, _fold_rows(jnp.where(causal, score, -jnp.inf), jnp.maximum, SUBLANES))
        return mn, mx

    last = n_chunks - 1
    n_pairs = n_chunks // 2
    odd = n_chunks % 2 == 1
    for h in range(N_IDX_HEADS):
        sta_ref[h] = idx_logits(h, 0)

    def idx_pair(pair, carry):
        j0 = 2 * pair
        carry = idx_reduce(j0, carry, sta_ref, stb_ref, j0 + 1)
        return idx_reduce(j0 + 1, carry, stb_ref, sta_ref, jnp.minimum(j0 + 2, last))

    extremes = lax.fori_loop(
        0, n_pairs, idx_pair,
        (jnp.full((SUBLANES, blk), jnp.inf, jnp.float32),
         jnp.full((SUBLANES, blk), -jnp.inf, jnp.float32)))
    mn8, mx8 = lax.cond(odd, lambda c: idx_reduce(last, c, sta_ref, None, None), lambda c: c,
                        extremes)
    smin = jnp.min(mn8, axis=0, keepdims=True)
    smax = jnp.max(mx8, axis=0, keepdims=True)

    def count_ge(thr_row):
        thr_b = jnp.broadcast_to(thr_row, (COUNT_ROWS, blk))

        def body(j, cnt):
            s = sc_ref[j]
            for r in range(0, blk, COUNT_ROWS):
                cnt = cnt + jnp.where(s[r:r + COUNT_ROWS] >= thr_b, 1.0, 0.0)
            return cnt

        cnt = lax.fori_loop(0, n_chunks, body, jnp.zeros((COUNT_ROWS, blk), jnp.float32))
        return jnp.sum(cnt, axis=0, keepdims=True)

    def bracket_extremes(lo_row, hi_row):
        lo_b = jnp.broadcast_to(lo_row, (SUBLANES, blk))
        hi_b = jnp.broadcast_to(hi_row, (SUBLANES, blk))

        def body(j, carry):
            a, b = carry
            s = sc_ref[j]
            for r in range(0, blk, SUBLANES):
                sr = s[r:r + SUBLANES]
                a = jnp.minimum(a, jnp.where(sr >= lo_b, sr, jnp.inf))
                b = jnp.maximum(b, jnp.where(sr < hi_b, sr, -jnp.inf))
            return a, b

        a, b = lax.fori_loop(0, n_chunks, body,
                             (jnp.full((SUBLANES, blk), jnp.inf, jnp.float32),
                              jnp.full((SUBLANES, blk), -jnp.inf, jnp.float32)))
        return jnp.min(a, axis=0, keepdims=True), jnp.max(b, axis=0, keepdims=True)

    n_valid = (i * blk + 1 + lax.broadcasted_iota(jnp.int32, (1, blk), 1)).astype(jnp.float32)
    active = n_valid > kf

    def count_step(state):
        it, lo, hi, c_lo, c_hi = state
        mid_val = _f32_to_key(0.5 * _key_to_f32(lo) + 0.5 * _key_to_f32(hi))
        mid_key = (lo >> 1) + (hi >> 1) + (lo & hi & 1)
        cand = jnp.where(it % SEARCH_PERIOD == KEY_STEP, mid_key, mid_val)
        cand = jnp.minimum(jnp.maximum(cand, lo + 1), hi - 1)
        c = count_ge(_key_to_f32(cand))
        ge = c >= kf
        return (it + 1, jnp.where(ge, cand, lo), jnp.where(ge, hi, cand),
                jnp.where(ge, c, c_lo), jnp.where(ge, c_hi, c))

    def tighten_step(state):
        it, lo, hi, c_lo, c_hi = state
        a, b = bracket_extremes(_key_to_f32(lo), _key_to_f32(hi))
        return (it + 1, jnp.maximum(lo, _f32_to_key(a)),
                jnp.minimum(hi, _key_succ(_f32_to_key(b))), c_lo, c_hi)

    def unresolved(state):
        _, lo, hi, c_lo, _ = state
        return jnp.where(active & (c_lo != kf) & (_key_succ(lo) < hi), 1.0, 0.0)

    def search_cond(state):
        pending = lax.cond(state[0] < MIN_SEARCH_STEPS, lambda: jnp.float32(1.0),
                           lambda: jnp.max(unresolved(state)))
        return (state[0] < MAX_SEARCH_STEPS) & (pending > 0.0)

    def search_step(state):
        return lax.cond(state[0] % SEARCH_PERIOD == TIGHTEN_STEP, tighten_step, count_step, state)

    def search_body(state):
        return search_step(search_step(state))

    lo0 = _f32_to_key(smin)
    hi0 = _key_succ(_f32_to_key(smax))
    state = lax.while_loop(search_cond, search_body,
                           (jnp.int32(0), lo0, hi0, n_valid, jnp.zeros((1, blk), jnp.float32)))
    _, lo, _, c_lo, _ = state
    thr_row = _key_to_f32(jnp.where(active, lo, KEY_NEG_INF + 1))
    thr_ref[...] = jnp.broadcast_to(thr_row, (SUBLANES, blk))
    excess = jnp.where(active & (c_lo > kf), 1.0, 0.0)

    @pl.when(jnp.max(excess) > 0.0)
    def _():
        thr_f = jnp.broadcast_to(thr_ref[0:1, :], (blk, blk))
        thr_c = jnp.broadcast_to(thr_ref[0:1, :], (COUNT_ROWS, blk))
        earlier = jnp.where(qcol < krow, 1.0, 0.0).astype(jnp.bfloat16)

        def gt_body(j, cnt):
            s = sc_ref[j]
            for r in range(0, blk, COUNT_ROWS):
                cnt = cnt + jnp.where(s[r:r + COUNT_ROWS] > thr_c, 1.0, 0.0)
            return cnt

        n_gt = jnp.sum(lax.fori_loop(0, n_chunks, gt_body,
                                     jnp.zeros((COUNT_ROWS, blk), jnp.float32)),
                       axis=0, keepdims=True)
        room = kf - n_gt

        def tie_body(j, seen):
            s = sc_ref[j]
            eq = s == thr_f
            eq_f = jnp.where(eq, 1.0, 0.0)
            prefix = jnp.dot(earlier, eq_f.astype(jnp.bfloat16),
                             preferred_element_type=jnp.float32)
            keep = (s > thr_f) | (eq & ((prefix + seen) < room))
            sc_ref[j] = jnp.where(keep, 1.0, -jnp.inf)
            return seen + jnp.sum(eq_f, axis=0, keepdims=True)

        lax.fori_loop(0, n_chunks, tie_body, jnp.zeros((1, blk), jnp.float32))
        thr_ref[...] = jnp.zeros((SUBLANES, blk), jnp.float32)

    m_ref[...] = jnp.full(m_ref.shape, NEG_BIG, jnp.float32)
    l_ref[...] = jnp.zeros(l_ref.shape, jnp.float32)
    acc_ref[...] = jnp.zeros(acc_ref.shape, jnp.float32)
    thr_full = jnp.broadcast_to(thr_ref[0:1, :], (blk, blk))

    def all_rows(x8, op):
        return jnp.broadcast_to(op(x8, axis=0, keepdims=True), x8.shape)

    def qk(h, j):
        p = h // HEADS_PER_SLAB
        kc = k_ref[0, pl.ds(pl.multiple_of(j * blk, blk), blk), p * LANES:(p + 1) * LANES]
        return jnp.dot(kc, qm_ref[h], preferred_element_type=jnp.float32)

    def set_mask(j):
        bias_ref[...] = jnp.where(sc_ref[j] >= thr_full, 0.0, NEG_BIG)

    def attend(j, st_cur, st_next, j_next):
        if st_next is not None:
            set_mask(j_next)
        for h in range(N_HEADS):
            if st_next is not None:
                st_next[h] = bias_ref[...] + qk(h, j_next)
            probs, alphas = [], []
            for c0 in range(0, blk, LANES):
                s = st_cur[h, :, c0:c0 + LANES]
                m_old = m_ref[h, :, c0:c0 + LANES]
                m_new = jnp.maximum(m_old, all_rows(_fold_rows(s, jnp.maximum, SUBLANES), jnp.max))
                alphas.append(jnp.exp2(m_old - m_new))
                probs.append(jnp.exp2(s - jnp.tile(m_new, (blk // SUBLANES, 1))).astype(jnp.bfloat16))
                m_ref[h, :, c0:c0 + LANES] = m_new
            o = jnp.dot(vt_ref[0, j, h], jnp.concatenate(probs, axis=1),
                        preferred_element_type=jnp.float32)
            alpha = jnp.concatenate(alphas, axis=1)
            l_ref[h] = alpha * l_ref[h] + o[HEAD_DIM:HEAD_DIM + SUBLANES, :]
            acc_ref[h * HEAD_DIM:(h + 1) * HEAD_DIM, :] = (
                jnp.tile(alpha, (HEAD_DIM // SUBLANES, 1)) * acc_ref[h * HEAD_DIM:(h + 1) * HEAD_DIM, :]
                + o[0:HEAD_DIM, :])

    set_mask(0)
    for h in range(N_HEADS):
        sta_ref[h] = bias_ref[...] + qk(h, 0)

    def pair_body(pair, carry):
        j0 = 2 * pair
        attend(j0, sta_ref, stb_ref, j0 + 1)
        attend(j0 + 1, stb_ref, sta_ref, jnp.minimum(j0 + 2, last))
        return carry

    lax.fori_loop(0, n_pairs, pair_body, 0)

    @pl.when(odd)
    def _():
        attend(last, sta_ref, None, None)

    for h in range(N_HEADS):
        acc_ref[h * HEAD_DIM:(h + 1) * HEAD_DIM, :] = (
            acc_ref[h * HEAD_DIM:(h + 1) * HEAD_DIM, :]
            / jnp.tile(l_ref[h], (HEAD_DIM // SUBLANES, 1)))
    o_ref[0] = (acc_ref[...].T * ga_ref[0].astype(jnp.float32)).astype(o_ref.dtype)


def _dsa(qt, qit, wt, k, ki2, vt, ga, *, topk):
    B, Lp, d_attn = k.shape
    blk = SEQ_BLOCK
    grid = (B, Lp // blk)
    qcol = lambda rows: pl.BlockSpec((1, rows, blk), lambda b, i: (b, 0, i))
    seq = lambda width: pl.BlockSpec((1, Lp, width), lambda b, i: (b, 0, 0))
    qrow = lambda width: pl.BlockSpec((1, blk, width), lambda b, i: (b, i, 0))
    return pl.pallas_call(
        functools.partial(_dsa_kernel, blk=blk, topk=topk),
        grid=grid,
        in_specs=[qcol(d_attn), qcol(d_attn), qcol(N_IDX_HEADS), seq(d_attn), seq(LANES),
                  pl.BlockSpec((1, Lp // blk, N_HEADS, VT_ROWS, blk), lambda b, i: (b, 0, 0, 0, 0)),
                  qrow(d_attn)],
        out_specs=qrow(d_attn),
        out_shape=jax.ShapeDtypeStruct((B, Lp, d_attn), jnp.bfloat16),
        scratch_shapes=[
            pltpu.VMEM((Lp // blk, blk, blk), jnp.float32),
            pltpu.VMEM((N_HEADS, LANES, blk), jnp.bfloat16),
            pltpu.VMEM((N_IDX_HEADS, LANES, blk), jnp.bfloat16),
            pltpu.VMEM((blk, blk), jnp.float32),
            pltpu.VMEM((N_HEADS, SUBLANES, blk), jnp.float32),
            pltpu.VMEM((N_HEADS, SUBLANES, blk), jnp.float32),
            pltpu.VMEM((d_attn, blk), jnp.float32),
            pltpu.VMEM((SUBLANES, blk), jnp.float32),
            pltpu.VMEM((N_HEADS, blk, blk), jnp.float32),
            pltpu.VMEM((N_HEADS, blk, blk), jnp.float32),
        ],
        compiler_params=pltpu.CompilerParams(
            dimension_semantics=("arbitrary", "arbitrary"), vmem_limit_bytes=VMEM_LIMIT),
        name="dsa",
    )(qt, qit, wt, k, ki2, vt, ga)


def _out_proj_kernel(yc_ref, ya_ref, h_ref, w_ref, g_ref, b_ref, o_ref, *, d_conv, alpha):
    y = jnp.dot(yc_ref[...], w_ref[0:d_conv, :], preferred_element_type=jnp.float32)
    y = y + jnp.dot(ya_ref[...], w_ref[d_conv:, :], preferred_element_type=jnp.float32)
    o_ref[...] = _layer_norm(alpha * h_ref[...] + y, g_ref[...], b_ref[...])


def _out_proj(yc, ya, h, w, g, b, *, tm, alpha):
    M, D = h.shape
    d_conv = yc.shape[1]
    row = lambda width: pl.BlockSpec((tm, width), lambda t: (t, 0))
    full = lambda shape: pl.BlockSpec(shape, lambda t: (0,) * len(shape))
    return pl.pallas_call(
        functools.partial(_out_proj_kernel, d_conv=d_conv, alpha=alpha),
        grid=(M // tm,),
        in_specs=[row(d_conv), row(ya.shape[1]), row(D), full(w.shape), full(g.shape),
                  full(b.shape)],
        out_specs=row(D),
        out_shape=jax.ShapeDtypeStruct((M, D), jnp.float32),
        compiler_params=pltpu.CompilerParams(
            dimension_semantics=("arbitrary",), vmem_limit_bytes=VMEM_LIMIT),
        name="out_proj",
    )(yc, ya, h, w, g, b)


def _row_tile(n, target, multiple):
    best = multiple
    for t in range(multiple, target + 1, multiple):
        if n % t == 0:
            best = t
    assert n % best == 0
    return best


def _rope_tables(length):
    inv_freq = ROPE_THETA ** (-jnp.arange(0, HEAD_DIM, 2, dtype=jnp.float32) / HEAD_DIM)
    ang = jnp.arange(length, dtype=jnp.float32)[:, None] * inv_freq[None, :]
    cos, sin = jnp.cos(ang), jnp.sin(ang)
    cos_l = jnp.tile(cos, (1, LANES // (HEAD_DIM // 2)))
    sin_l = jnp.tile(jnp.concatenate([-sin, sin], axis=1), (1, LANES // HEAD_DIM))
    return cos_l, sin_l, cos.T, sin.T


def kernel(x, meta_tokens, w_in, conv_w, conv_b, conv_ln_g, conv_ln_b, w_out,
           post_ln_g, post_ln_b):
    B, S, D = x.shape
    depth = w_in.shape[0]
    d_conv = conv_w.shape[2]
    d_attn = N_HEADS * HEAD_DIM
    L = S + N_META
    topk = min(TOPK_MAX, L // 4)
    Lp = -(-L // SEQ_BLOCK) * SEQ_BLOCK
    alpha = (2.0 * depth) ** 0.25
    q_scale = (HEAD_DIM ** -0.5) * LOG2_E
    idx_scale = (IDX_DIM ** -0.5) * (N_IDX_HEADS ** -0.5)

    meta = jnp.broadcast_to(meta_tokens[None].astype(x.dtype), (B, N_META, D))
    h = jnp.concatenate([meta, x, jnp.zeros((B, Lp - L, D), x.dtype)], axis=1)
    cos, sin, cos_t, sin_t = _rope_tables(Lp)

    bounds = np.cumsum([0, d_conv, d_conv, d_conv, d_attn, d_attn, d_attn, d_attn,
                        N_IDX_HEADS * IDX_DIM, IDX_DIM, N_IDX_HEADS])
    part = lambda n: w_in[:, :, bounds[n]:bounds[n + 1]]
    bf = jnp.bfloat16
    w_nat = jnp.concatenate(
        [part(0), part(1), part(2), part(4), part(6)] + [part(8)] * (LANES // IDX_DIM),
        axis=2).astype(bf)
    w_tr = jnp.swapaxes(jnp.concatenate(
        [part(3), part(7), part(5), part(9),
         jnp.zeros((depth, D, 2 * SUBLANES - N_IDX_HEADS), w_in.dtype)], axis=2), 1, 2).astype(bf)
    w_out_b = w_out.astype(bf)

    tm_out = _row_tile(B * Lp, 1024, 16)

    for l in range(depth):
        yc, k, ki2, ga, qt, qit, vt, wt = _in_proj(
            h, w_nat[l], w_tr[l], conv_w[l], conv_b[l][None], conv_ln_g[l][None],
            conv_ln_b[l][None], cos, sin, cos_t, sin_t, q_scale=q_scale, idx_scale=idx_scale)
        ya = _dsa(qt, qit, wt, k, ki2, vt, ga, topk=topk)
        h = _out_proj(yc.reshape(B * Lp, d_conv), ya.reshape(B * Lp, d_attn),
                      h.reshape(B * Lp, D), w_out_b[l], post_ln_g[l][None], post_ln_b[l][None],
                      tm=tm_out, alpha=alpha).reshape(B, Lp, D)

    return h[:, N_META:L]
```
